```python
import math
import jax, jax.numpy as jnp
from jax import lax
import numpy as np

D_MODEL = 2048
BATCH = 4
SEQ = 4096
DEPTH = 2

CHUNK = 64
N_A = DEPTH // 2
N_B = DEPTH - N_A
POOL_WINDOWS = (2, 4, 8, 16)
N_POOL_GROUPS = len(POOL_WINDOWS)
POOL_GROUP_DIM = D_MODEL // N_POOL_GROUPS
HEAD_DIM = 128
N_HEADS = D_MODEL // HEAD_DIM
D_FF = 4 * D_MODEL
Q_BLOCK = 128
EPS = 1e-6

kernel_name = "yoco_pool_stickbreaking_trunk"


def rms_norm(x, g):
    xf = x.astype(jnp.float32)
    y = xf * lax.rsqrt(jnp.mean(xf * xf, axis=-1, keepdims=True) + EPS)
    return (y * g.astype(jnp.float32)).astype(x.dtype)


def multiscale_pool_mixer(x, w_pool, scale):
    B, S, D = x.shape
    xg = x.reshape(B, S, N_POOL_GROUPS, POOL_GROUP_DIM)
    pos = jnp.arange(S)
    outs = []
    for g, w in enumerate(POOL_WINDOWS):
        xf = xg[:, :, g, :].astype(jnp.float32)
        cs = jnp.cumsum(xf, axis=1)
        lag = jnp.pad(cs, ((0, 0), (w, 0), (0, 0)))[:, :S]
        cnt = jnp.minimum(pos + 1, w).astype(jnp.float32)[None, :, None]
        diff = ((cs - lag) / cnt - xf).astype(x.dtype)
        outs.append(jnp.einsum('bsc,ce->bse', diff, w_pool[g]))
    return jnp.concatenate(outs, axis=-1) * scale


def squared_relu_mlp(x, w_up, w_down):
    h = jax.nn.relu(jnp.einsum('bsd,df->bsf', x, w_up))
    return jnp.einsum('bsf,fd->bsd', h * h, w_down)


def stick_breaking_attention(q, k, v):
    B, S, H, Dh = q.shape
    inv_sqrt = 1.0 / math.sqrt(Dh)
    outs = []
    for i in range(S // Q_BLOCK):
        q0 = i * Q_BLOCK
        end = q0 + Q_BLOCK
        qb = q[:, q0:end]
        kb = k[:, :end]
        vb = v[:, :end]
        z = jnp.einsum('bthd,bshd->bhts', qb, kb).astype(jnp.float32) * inv_sqrt
        t_idx = q0 + jnp.arange(Q_BLOCK)
        s_idx = jnp.arange(end)
        causal = s_idx[None, :] < t_idx[:, None]
        log_beta = jax.nn.log_sigmoid(z)
        log_1m = jnp.where(causal, jax.nn.log_sigmoid(-z), 0.0)
        between = lax.cumsum(log_1m, axis=3, reverse=True) - log_1m
        a = jnp.where(causal, jnp.exp(log_beta + between), 0.0)
        outs.append(jnp.einsum('bhts,bshd->bthd', a.astype(vb.dtype), vb))
    return jnp.concatenate(outs, axis=1)


def setup_inputs(seed: int = 0) -> dict:
    key = jax.random.key(seed)
    ks = jax.random.split(key, 16)
    f32 = jnp.float32

    def nrm(k, shape, fan_in):
        return jax.random.normal(k, shape, f32) * (fan_in ** -0.5)

    def gain(k, shape):
        return 1.0 + 0.05 * jax.random.normal(k, shape, f32)

    x = jax.random.normal(ks[0], (BATCH, SEQ, D_MODEL), f32)
    pool_norm = gain(ks[1], (N_A, D_MODEL))
    pool_w = nrm(ks[2], (N_A, N_POOL_GROUPS, POOL_GROUP_DIM, POOL_GROUP_DIM), POOL_GROUP_DIM)
    pool_scale = 0.5 + 0.05 * jax.random.normal(ks[3], (N_A, D_MODEL), f32)
    kv_norm = gain(ks[4], (D_MODEL,))
    w_kv = nrm(ks[5], (D_MODEL, 2 * D_MODEL), D_MODEL)
    attn_norm = gain(ks[6], (N_B, D_MODEL))
    w_q = nrm(ks[7], (N_B, D_MODEL, D_MODEL), D_MODEL)
    w_o = nrm(ks[8], (N_B, D_MODEL, D_MODEL), D_MODEL)
    mlp_norm = gain(ks[9], (DEPTH, D_MODEL))
    w_up = nrm(ks[10], (DEPTH, D_MODEL, D_FF), D_MODEL)
    w_down = nrm(ks[11], (DEPTH, D_FF, D_MODEL), D_FF)
    final_norm = gain(ks[12], (D_MODEL,))
    return {"x": x, "pool_norm": pool_norm, "pool_w": pool_w, "pool_scale": pool_scale,
            "kv_norm": kv_norm, "w_kv": w_kv, "attn_norm": attn_norm, "w_q": w_q,
            "w_o": w_o, "mlp_norm": mlp_norm, "w_up": w_up, "w_down": w_down,
            "final_norm": final_norm}


def reference(x, pool_norm, pool_w, pool_scale, kv_norm, w_kv, attn_norm, w_q, w_o,
              mlp_norm, w_up, w_down, final_norm):
    B, S, D = x.shape
    k_shared = None
    v_shared = None
    for layer in range(DEPTH):
        if layer < N_A:
            x = x + multiscale_pool_mixer(rms_norm(x, pool_norm[layer]), pool_w[layer],
                                          pool_scale[layer])
        else:
            if layer == N_A:
                kv = jnp.einsum('bsd,de->bse', rms_norm(x, kv_norm), w_kv)
                kv = kv.reshape(B, S, 2, N_HEADS, HEAD_DIM)
                k_shared = kv[:, :, 0]
                v_shared = kv[:, :, 1]
            j = layer - N_A
            q = jnp.einsum('bsd,de->bse', rms_norm(x, attn_norm[j]), w_q[j])
            q = q.reshape(B, S, N_HEADS, HEAD_DIM)
            o = stick_breaking_attention(q, k_shared, v_shared).reshape(B, S, D)
            x = x + jnp.einsum('bsd,de->bse', o, w_o[j])
        x = x + squared_relu_mlp(rms_norm(x, mlp_norm[layer]), w_up[layer], w_down[layer])
    return rms_norm(x, final_norm)
```

```python
import functools
import math

import jax
import jax.numpy as jnp
from jax import lax
from jax.experimental import pallas as pl
from jax.experimental.pallas import tpu as pltpu

_EPS = 1e-6
_POOL_WINDOWS = (2, 4, 8, 16)
_HEAD_DIM = 128
_POOL_HALO = 16
_ATTN_BLOCK = 128
_LOG_WEIGHT_FLOOR = -110.0
_VMEM_LIMIT_BYTES = 56 * 1024 * 1024

_F32 = jnp.float32
_BF16 = jnp.bfloat16


def _inv_rms(x):
    return lax.rsqrt(jnp.mean(x * x, axis=-1, keepdims=True) + _EPS)


def _pool_kernel(x_ref, halo_ref, g_ref, w_ref, sc_ref, o_ref, *, ts, group_dim):
    i = pl.program_id(1)
    g = g_ref[...]
    x = x_ref[0]
    xn = x * _inv_rms(x) * g
    h = halo_ref[0]
    hn = h * _inv_rms(h) * g
    hn = jnp.where(i > 0, hn, 0.0)
    ext = jnp.concatenate([hn, xn], axis=0)
    pos = i * ts + lax.broadcasted_iota(jnp.int32, (ts, 1), 0)
    for gi, w in enumerate(_POOL_WINDOWS):
        c0 = gi * group_dim
        s = ext[:, c0:c0 + group_dim]
        k = 1
        while k < w:
            s = s + pltpu.roll(s, k, axis=0)
            k *= 2
        win = s[_POOL_HALO:]
        inv_cnt = 1.0 / jnp.minimum(pos + 1, w).astype(_F32)
        diff = win * inv_cnt - xn[:, c0:c0 + group_dim]
        y = jnp.dot(diff.astype(_BF16), w_ref[gi], preferred_element_type=_F32)
        o_ref[0, :, c0:c0 + group_dim] = x[:, c0:c0 + group_dim] + y * sc_ref[:, c0:c0 + group_dim]


def _pool_mixer(x, norm_g, w_bf16, scale, *, ts=512):
    b, s, d = x.shape
    n_groups = len(_POOL_WINDOWS)
    group_dim = d // n_groups
    halo_blocks_per_tile = ts // _POOL_HALO
    kern = functools.partial(_pool_kernel, ts=ts, group_dim=group_dim)
    return pl.pallas_call(
        kern,
        grid=(b, s // ts),
        in_specs=[
            pl.BlockSpec((1, ts, d), lambda bi, i: (bi, i, 0)),
            pl.BlockSpec((1, _POOL_HALO, d),
                         lambda bi, i: (bi, jnp.maximum(i * halo_blocks_per_tile - 1, 0), 0)),
            pl.BlockSpec((1, d), lambda bi, i: (0, 0)),
            pl.BlockSpec((n_groups, group_dim, group_dim), lambda bi, i: (0, 0, 0)),
            pl.BlockSpec((1, d), lambda bi, i: (0, 0)),
        ],
        out_specs=pl.BlockSpec((1, ts, d), lambda bi, i: (bi, i, 0)),
        out_shape=jax.ShapeDtypeStruct((b, s, d), _F32),
        compiler_params=pltpu.CompilerParams(
            dimension_semantics=("parallel", "arbitrary"),
            vmem_limit_bytes=_VMEM_LIMIT_BYTES),
    )(x, x, norm_g.reshape(1, d), w_bf16, scale.reshape(1, d))


def _mlp_kernel(x_ref, g_ref, wu_ref, wd_ref, *rest, final_norm):
    if final_norm:
        gf_ref, o_ref, xn_ref = rest
    else:
        o_ref, xn_ref = rest
    f = pl.program_id(1)

    @pl.when(f == 0)
    def _():
        x = x_ref[...]
        xn_ref[...] = (x * _inv_rms(x) * g_ref[...]).astype(_BF16)

    h = jnp.dot(xn_ref[...], wu_ref[...], preferred_element_type=_F32)
    h = jnp.maximum(h, 0.0)
    h = (h * h).astype(_BF16)
    y = jnp.dot(h, wd_ref[...], preferred_element_type=_F32)

    @pl.when(f == 0)
    def _():
        o_ref[...] = x_ref[...] + y

    @pl.when(f > 0)
    def _():
        o_ref[...] += y

    if final_norm:
        @pl.when(f == pl.num_programs(1) - 1)
        def _():
            r = o_ref[...]
            o_ref[...] = r * _inv_rms(r) * gf_ref[...]


def _mlp(x2d, norm_g, w_up, w_down, final_g=None, *, tm=512, tf=512):
    m, d = x2d.shape
    d_ff = w_up.shape[1]
    final_norm = final_g is not None
    in_specs = [
        pl.BlockSpec((tm, d), lambda i, f: (i, 0)),
        pl.BlockSpec((1, d), lambda i, f: (0, 0)),
        pl.BlockSpec((d, tf), lambda i, f: (0, f)),
        pl.BlockSpec((tf, d), lambda i, f: (f, 0)),
    ]
    args = [x2d, norm_g.reshape(1, d), w_up, w_down]
    if final_norm:
        in_specs.append(pl.BlockSpec((1, d), lambda i, f: (0, 0)))
        args.append(final_g.reshape(1, d))
    return pl.pallas_call(
        functools.partial(_mlp_kernel, final_norm=final_norm),
        grid=(m // tm, d_ff // tf),
        in_specs=in_specs,
        out_specs=pl.BlockSpec((tm, d), lambda i, f: (i, 0)),
        out_shape=jax.ShapeDtypeStruct((m, d), _F32),
        scratch_shapes=[pltpu.VMEM((tm, d), _BF16)],
        compiler_params=pltpu.CompilerParams(
            dimension_semantics=("parallel", "arbitrary"),
            vmem_limit_bytes=_VMEM_LIMIT_BYTES),
    )(*args)


def _proj_kernel(x_ref, gkv_ref, gq_ref, w_ref, o_ref, xkv_ref, xq_ref, *, n_kv_blocks):
    n = pl.program_id(1)

    @pl.when(n == 0)
    def _():
        x = x_ref[...]
        xh = x * _inv_rms(x)
        xkv_ref[...] = (xh * gkv_ref[...]).astype(_BF16)
        xq_ref[...] = (xh * gq_ref[...]).astype(_BF16)

    @pl.when(n < n_kv_blocks)
    def _():
        o_ref[...] = jnp.dot(xkv_ref[...], w_ref[...], preferred_element_type=_F32).astype(_BF16)

    @pl.when(n >= n_kv_blocks)
    def _():
        o_ref[...] = jnp.dot(xq_ref[...], w_ref[...], preferred_element_type=_F32).astype(_BF16)


def _kvq_proj(x2d, kv_g, q_g, w_cat, n_kv_cols, *, tm=512, tn=512):
    m, d = x2d.shape
    n_total = w_cat.shape[1]
    return pl.pallas_call(
        functools.partial(_proj_kernel, n_kv_blocks=n_kv_cols // tn),
        grid=(m // tm, n_total // tn),
        in_specs=[
            pl.BlockSpec((tm, d), lambda i, n: (i, 0)),
            pl.BlockSpec((1, d), lambda i, n: (0, 0)),
            pl.BlockSpec((1, d), lambda i, n: (0, 0)),
            pl.BlockSpec((d, tn), lambda i, n: (0, n)),
        ],
        out_specs=pl.BlockSpec((tm, tn), lambda i, n: (i, n)),
        out_shape=jax.ShapeDtypeStruct((m, n_total), _BF16),
        scratch_shapes=[pltpu.VMEM((tm, d), _BF16), pltpu.VMEM((tm, d), _BF16)],
        compiler_params=pltpu.CompilerParams(
            dimension_semantics=("parallel", "arbitrary"),
            vmem_limit_bytes=_VMEM_LIMIT_BYTES),
    )(x2d, kv_g.reshape(1, d), q_g.reshape(1, d), w_cat)


def _attn_kernel(q_ref, k_ref, v_ref, o_ref, *, seq, inv_sqrt):
    t = _ATTN_BLOCK
    row = lax.broadcasted_iota(jnp.int32, (t, t), 0)
    col = lax.broadcasted_iota(jnp.int32, (t, t), 1)
    strictly_causal = col < row
    suffix = jnp.where(row >= col, 1.0, 0.0).astype(_BF16)
    suffix2 = jnp.concatenate([suffix, suffix], axis=0)

    def tile(q, k0, carry, acc, diag):
        k = k_ref[0, pl.ds(k0, t), :]
        v = v_ref[0, pl.ds(k0, t), :]
        z = lax.dot_general(q, k, (((1,), (1,)), ((), ())), preferred_element_type=_F32) * inv_sqrt
        sp = jnp.log(1.0 + jnp.exp(-jnp.abs(z)))
        log_beta = jnp.minimum(z, 0.0) - sp
        log_1m = jnp.minimum(-z, 0.0) - sp
        if diag:
            log_1m = jnp.where(strictly_causal, log_1m, 0.0)
        hi = log_1m.astype(_BF16)
        lo = (log_1m - hi.astype(_F32)).astype(_BF16)
        incl = jnp.dot(jnp.concatenate([hi, lo], axis=1), suffix2, preferred_element_type=_F32)
        between = incl - log_1m + carry
        a = jnp.exp(log_beta + between)
        if diag:
            a = jnp.where(strictly_causal, a, 0.0)
        acc = acc + jnp.dot(a.astype(_BF16), v, preferred_element_type=_F32)
        carry = carry + incl[:, 0:1]
        return carry, acc

    def q_block(i, _):
        q0 = pl.multiple_of(i * t, t)
        q = q_ref[0, pl.ds(q0, t), :]
        carry, acc = tile(q, q0, jnp.zeros((t, 1), _F32), jnp.zeros((t, _HEAD_DIM), _F32), True)

        def cond(state):
            j, cmax, _, _ = state
            return jnp.logical_and(j >= 0, cmax > _LOG_WEIGHT_FLOOR)

        def body(state):
            j, _, carry, acc = state
            carry, acc = tile(q, pl.multiple_of(j * t, t), carry, acc, False)
            return j - 1, jnp.max(carry), carry, acc

        _, _, _, acc = lax.while_loop(cond, body, (i - 1, jnp.max(carry), carry, acc))
        o_ref[0, pl.ds(q0, t), :] = acc.astype(o_ref.dtype)
        return 0

    lax.fori_loop(0, seq // t, q_block, 0)


def _stick_breaking_attention(kvq, b, s, d, n_heads):
    kern = functools.partial(_attn_kernel, seq=s, inv_sqrt=1.0 / math.sqrt(_HEAD_DIM))
    return pl.pallas_call(
        kern,
        grid=(b, n_heads),
        in_specs=[
            pl.BlockSpec((1, s, _HEAD_DIM), lambda bi, h: (bi, 0, 2 * n_heads + h)),
            pl.BlockSpec((1, s, _HEAD_DIM), lambda bi, h: (bi, 0, h)),
            pl.BlockSpec((1, s, _HEAD_DIM), lambda bi, h: (bi, 0, n_heads + h)),
        ],
        out_specs=pl.BlockSpec((1, s, _HEAD_DIM), lambda bi, h: (bi, 0, h)),
        out_shape=jax.ShapeDtypeStruct((b, s, d), _BF16),
        compiler_params=pltpu.CompilerParams(
            dimension_semantics=("parallel", "arbitrary"),
            vmem_limit_bytes=_VMEM_LIMIT_BYTES),
    )(kvq, kvq, kvq)


def _oproj_kernel(x_ref, o_ref_in, w_ref, out_ref):
    out_ref[...] = x_ref[...] + jnp.dot(o_ref_in[...], w_ref[...], preferred_element_type=_F32)


def _out_proj(x2d, o2d, w_o, *, tm=512):
    m, d = x2d.shape
    return pl.pallas_call(
        _oproj_kernel,
        grid=(m // tm,),
        in_specs=[
            pl.BlockSpec((tm, d), lambda i: (i, 0)),
            pl.BlockSpec((tm, d), lambda i: (i, 0)),
            pl.BlockSpec((d, d), lambda i: (0, 0)),
        ],
        out_specs=pl.BlockSpec((tm, d), lambda i: (i, 0)),
        out_shape=jax.ShapeDtypeStruct((m, d), _F32),
        compiler_params=pltpu.CompilerParams(
            dimension_semantics=("parallel",),
            vmem_limit_bytes=_VMEM_LIMIT_BYTES),
    )(x2d, o2d, w_o)


def kernel(x, pool_norm, pool_w, pool_scale, kv_norm, w_kv, attn_norm, w_q, w_o, mlp_norm,
           w_up, w_down, final_norm):
    b, s, d = x.shape
    n_heads = d // _HEAD_DIM
    m = b * s
    assert pool_norm.shape[0] == 1 and attn_norm.shape[0] == 1 and mlp_norm.shape[0] == 2

    x1 = _pool_mixer(x, pool_norm[0], pool_w[0].astype(_BF16), pool_scale[0])
    x2 = _mlp(x1.reshape(m, d), mlp_norm[0], w_up[0].astype(_BF16), w_down[0].astype(_BF16))

    w_cat = jnp.concatenate([w_kv, w_q[0]], axis=1).astype(_BF16)
    kvq = _kvq_proj(x2, kv_norm, attn_norm[0], w_cat, 2 * d)
    o = _stick_breaking_attention(kvq.reshape(b, s, 3 * d), b, s, d, n_heads)
    x3 = _out_proj(x2, o.reshape(m, d), w_o[0].astype(_BF16))

    out = _mlp(x3, mlp_norm[1], w_up[1].astype(_BF16), w_down[1].astype(_BF16), final_norm)
    return out.reshape(b, s, d)
```

```python
import functools
import math

import jax
import jax.numpy as jnp
from jax import lax
from jax.experimental import pallas as pl
from jax.experimental.pallas import tpu as pltpu

_EPS = 1e-6
_POOL_WINDOWS = (2, 4, 8, 16)
_HEAD_DIM = 128
_POOL_HALO = 16
_ATTN_BLOCK = 128
_ATTN_BACK_BLOCKS = 2
_LOG2_WEIGHT_FLOOR = -160.0
_VMEM_LIMIT_BYTES = 56 * 1024 * 1024

_F32 = jnp.float32
_BF16 = jnp.bfloat16


def _inv_rms(x):
    return lax.rsqrt(jnp.mean(x * x, axis=-1, keepdims=True) + _EPS)


def _pool_kernel(x_ref, halo_ref, g_ref, w_ref, sc_ref, o_ref, *, ts, group_dim):
    i = pl.program_id(1)
    g = g_ref[...]
    x = x_ref[0]
    xn = x * _inv_rms(x) * g
    h = halo_ref[0]
    hn = h * _inv_rms(h) * g
    hn = jnp.where(i > 0, hn, 0.0)
    ext = jnp.concatenate([hn, xn], axis=0)
    pos = i * ts + lax.broadcasted_iota(jnp.int32, (ts, 1), 0)
    for gi, w in enumerate(_POOL_WINDOWS):
        c0 = gi * group_dim
        s = ext[:, c0:c0 + group_dim]
        k = 1
        while k < w:
            s = s + pltpu.roll(s, k, axis=0)
            k *= 2
        win = s[_POOL_HALO:]
        inv_cnt = 1.0 / jnp.minimum(pos + 1, w).astype(_F32)
        diff = win * inv_cnt - xn[:, c0:c0 + group_dim]
        y = jnp.dot(diff.astype(_BF16), w_ref[gi], preferred_element_type=_F32)
        o_ref[0, :, c0:c0 + group_dim] = x[:, c0:c0 + group_dim] + y * sc_ref[:, c0:c0 + group_dim]


def _pool_mixer(x, norm_g, w_bf16, scale, *, ts=512):
    b, s, d = x.shape
    n_groups = len(_POOL_WINDOWS)
    group_dim = d // n_groups
    halo_blocks_per_tile = ts // _POOL_HALO
    kern = functools.partial(_pool_kernel, ts=ts, group_dim=group_dim)
    return pl.pallas_call(
        kern,
        grid=(b, s // ts),
        in_specs=[
            pl.BlockSpec((1, ts, d), lambda bi, i: (bi, i, 0)),
            pl.BlockSpec((1, _POOL_HALO, d),
                         lambda bi, i: (bi, jnp.maximum(i * halo_blocks_per_tile - 1, 0), 0)),
            pl.BlockSpec((1, d), lambda bi, i: (0, 0)),
            pl.BlockSpec((n_groups, group_dim, group_dim), lambda bi, i: (0, 0, 0)),
            pl.BlockSpec((1, d), lambda bi, i: (0, 0)),
        ],
        out_specs=pl.BlockSpec((1, ts, d), lambda bi, i: (bi, i, 0)),
        out_shape=jax.ShapeDtypeStruct((b, s, d), _F32),
        compiler_params=pltpu.CompilerParams(
            dimension_semantics=("parallel", "arbitrary"),
            vmem_limit_bytes=_VMEM_LIMIT_BYTES),
    )(x, x, norm_g.reshape(1, d), w_bf16, scale.reshape(1, d))


def _mlp_kernel(x_ref, g_ref, wu_ref, wd_ref, *rest, final_norm):
    if final_norm:
        gf_ref, o_ref, xn_ref = rest
    else:
        o_ref, xn_ref = rest
    f = pl.program_id(1)

    @pl.when(f == 0)
    def _():
        x = x_ref[...]
        xn_ref[...] = (x * _inv_rms(x) * g_ref[...]).astype(_BF16)

    h = jnp.dot(xn_ref[...], wu_ref[...], preferred_element_type=_F32)
    h = jnp.maximum(h, 0.0)
    h = (h * h).astype(_BF16)
    y = jnp.dot(h, wd_ref[...], preferred_element_type=_F32)

    @pl.when(f == 0)
    def _():
        o_ref[...] = x_ref[...] + y

    @pl.when(f > 0)
    def _():
        o_ref[...] += y

    if final_norm:
        @pl.when(f == pl.num_programs(1) - 1)
        def _():
            r = o_ref[...]
            o_ref[...] = r * _inv_rms(r) * gf_ref[...]


def _mlp(x2d, norm_g, w_up, w_down, final_g=None, *, tm=512, tf=512):
    m, d = x2d.shape
    d_ff = w_up.shape[1]
    final_norm = final_g is not None
    in_specs = [
        pl.BlockSpec((tm, d), lambda i, f: (i, 0)),
        pl.BlockSpec((1, d), lambda i, f: (0, 0)),
        pl.BlockSpec((d, tf), lambda i, f: (0, f)),
        pl.BlockSpec((tf, d), lambda i, f: (f, 0)),
    ]
    args = [x2d, norm_g.reshape(1, d), w_up, w_down]
    if final_norm:
        in_specs.append(pl.BlockSpec((1, d), lambda i, f: (0, 0)))
        args.append(final_g.reshape(1, d))
    return pl.pallas_call(
        functools.partial(_mlp_kernel, final_norm=final_norm),
        grid=(m // tm, d_ff // tf),
        in_specs=in_specs,
        out_specs=pl.BlockSpec((tm, d), lambda i, f: (i, 0)),
        out_shape=jax.ShapeDtypeStruct((m, d), _F32),
        scratch_shapes=[pltpu.VMEM((tm, d), _BF16)],
        compiler_params=pltpu.CompilerParams(
            dimension_semantics=("parallel", "arbitrary"),
            vmem_limit_bytes=_VMEM_LIMIT_BYTES),
    )(*args)


def _proj_kernel(x_ref, gkv_ref, gq_ref, w_ref, o_ref, xkv_ref, xq_ref, *, n_kv_blocks, q_scale):
    n = pl.program_id(1)

    @pl.when(n == 0)
    def _():
        x = x_ref[...]
        xh = x * _inv_rms(x)
        xkv_ref[...] = (xh * gkv_ref[...]).astype(_BF16)
        xq_ref[...] = (xh * gq_ref[...]).astype(_BF16)

    @pl.when(n < n_kv_blocks)
    def _():
        o_ref[...] = jnp.dot(xkv_ref[...], w_ref[...], preferred_element_type=_F32).astype(_BF16)

    @pl.when(n >= n_kv_blocks)
    def _():
        q = jnp.dot(xq_ref[...], w_ref[...], preferred_element_type=_F32)
        o_ref[...] = (q * q_scale).astype(_BF16)


def _kvq_proj(x2d, kv_g, q_g, w_cat, n_kv_cols, q_scale, *, tm=512, tn=512):
    m, d = x2d.shape
    n_total = w_cat.shape[1]
    return pl.pallas_call(
        functools.partial(_proj_kernel, n_kv_blocks=n_kv_cols // tn, q_scale=q_scale),
        grid=(m // tm, n_total // tn),
        in_specs=[
            pl.BlockSpec((tm, d), lambda i, n: (i, 0)),
            pl.BlockSpec((1, d), lambda i, n: (0, 0)),
            pl.BlockSpec((1, d), lambda i, n: (0, 0)),
            pl.BlockSpec((d, tn), lambda i, n: (0, n)),
        ],
        out_specs=pl.BlockSpec((tm, tn), lambda i, n: (i, n)),
        out_shape=jax.ShapeDtypeStruct((m, n_total), _BF16),
        scratch_shapes=[pltpu.VMEM((tm, d), _BF16), pltpu.VMEM((tm, d), _BF16)],
        compiler_params=pltpu.CompilerParams(
            dimension_semantics=("parallel", "arbitrary"),
            vmem_limit_bytes=_VMEM_LIMIT_BYTES),
    )(x2d, kv_g.reshape(1, d), q_g.reshape(1, d), w_cat)


def _log2_sigmoid(z2):
    return jnp.minimum(z2, 0.0) - jnp.log2(1.0 + jnp.exp2(-jnp.abs(z2)))


def _split_hi_lo(x):
    hi = x.astype(_BF16)
    lo = (x - hi.astype(_F32)).astype(_BF16)
    return jnp.concatenate([hi, lo], axis=1)


def _attn_kernel(q_ref, k_ref, v_ref, o_ref, lb_ref, lhs_ref, cs_ref, *, seq):
    t = _ATTN_BLOCK
    n_blocks = seq // t
    row = lax.broadcasted_iota(jnp.int32, (t, t), 0)
    col = lax.broadcasted_iota(jnp.int32, (t, t), 1)
    strictly_causal = col < row
    later = jnp.where(row > col, 1.0, 0.0).astype(_BF16)
    ones = jnp.ones((t, t), _BF16)
    sums = jnp.concatenate([jnp.concatenate([later, ones], axis=1)] * 2, axis=0)
    nt_dims = (((1,), (1,)), ((), ()))

    n_win = _ATTN_BACK_BLOCKS + 1

    def logs_phase(i, q0, n_back):
        w0 = q0 - n_back * t
        z_all = lax.dot_general(q_ref[0, pl.ds(q0, t), :], k_ref[0, pl.ds(w0, (n_back + 1) * t), :],
                                nt_dims, preferred_element_type=_F32)
        for jb in range(n_back + 1):
            z2 = z_all[:, (n_back - jb) * t:(n_back - jb + 1) * t]
            lb = _log2_sigmoid(z2)
            l1 = lb - z2
            if jb == 0:
                l1 = jnp.where(strictly_causal, l1, 0.0)
            lb_ref[i, :, jb * t:(jb + 1) * t] = lb
            lhs_ref[pl.ds((i * n_win + jb) * t, t), :] = _split_hi_lo(l1)
        for jb in range(n_back + 1, n_win):
            lhs_ref[pl.ds((i * n_win + jb) * t, t), :] = jnp.zeros((t, 2 * t), _BF16)

    def weights_phase(i, q0, n_back):
        w0 = q0 - n_back * t
        carry = None
        weights = []
        for jb in range(n_back + 1):
            rows = pl.ds((i * n_win + jb) * t, t)
            between = cs_ref[rows, :t] if carry is None else cs_ref[rows, :t] + carry
            a = jnp.exp2(lb_ref[i, :, jb * t:(jb + 1) * t] + between)
            if jb == 0:
                a = jnp.where(strictly_causal, a, 0.0)
            weights.append(a.astype(_BF16))
            carry = cs_ref[rows, t:] if carry is None else carry + cs_ref[rows, t:]
        a_all = jnp.concatenate(weights[::-1], axis=1)
        out = jnp.dot(a_all, v_ref[0, pl.ds(w0, (n_back + 1) * t), :], preferred_element_type=_F32)
        o_ref[0, pl.ds(q0, t), :] = out.astype(o_ref.dtype)
        return carry

    def run_phase(phase, init, unroll):
        for i in range(_ATTN_BACK_BLOCKS):
            phase(i, i * t, i)

        def step(i, acc):
            r = phase(i, pl.multiple_of(i * t, t), _ATTN_BACK_BLOCKS)
            return acc if r is None else jnp.maximum(acc, r)

        return lax.fori_loop(_ATTN_BACK_BLOCKS, n_blocks, step, init, unroll=unroll)

    run_phase(logs_phase, 0, unroll=5)
    cs_ref[...] = jnp.dot(lhs_ref[...], sums, preferred_element_type=_F32)
    worst = run_phase(weights_phase, jnp.full((t, t), -jnp.inf, _F32), unroll=5)

    @pl.when(jnp.max(worst) > _LOG2_WEIGHT_FLOOR)
    def _():
        def tile(q, k0, carry, acc, diag):
            z2 = lax.dot_general(q, k_ref[0, pl.ds(k0, t), :], nt_dims, preferred_element_type=_F32)
            lb = _log2_sigmoid(z2)
            l1 = lb - z2
            if diag:
                l1 = jnp.where(strictly_causal, l1, 0.0)
            cs = jnp.dot(_split_hi_lo(l1), sums, preferred_element_type=_F32)
            a = jnp.exp2(lb + cs[:, :t] + carry)
            if diag:
                a = jnp.where(strictly_causal, a, 0.0)
            acc = acc + jnp.dot(a.astype(_BF16), v_ref[0, pl.ds(k0, t), :], preferred_element_type=_F32)
            return carry + cs[:, t:], acc

        def q_block(i, _):
            q0 = pl.multiple_of(i * t, t)
            q = q_ref[0, pl.ds(q0, t), :]
            carry, acc = tile(q, q0, jnp.zeros((t, t), _F32), jnp.zeros((t, _HEAD_DIM), _F32), True)

            def cond(state):
                j, cmax, _, _ = state
                return jnp.logical_and(j >= 0, cmax > _LOG2_WEIGHT_FLOOR)

            def body(state):
                j, _, carry, acc = state
                carry, acc = tile(q, pl.multiple_of(j * t, t), carry, acc, False)
                return j - 1, jnp.max(carry), carry, acc

            _, _, _, acc = lax.while_loop(cond, body, (i - 1, jnp.max(carry), carry, acc))
            o_ref[0, pl.ds(q0, t), :] = acc.astype(o_ref.dtype)
            return 0

        lax.fori_loop(0, n_blocks, q_block, 0)


def _stick_breaking_attention(kvq, b, s, d, n_heads):
    t, n_win = _ATTN_BLOCK, _ATTN_BACK_BLOCKS + 1
    n_blocks = s // t
    assert s % t == 0 and n_blocks >= n_win
    return pl.pallas_call(
        functools.partial(_attn_kernel, seq=s),
        grid=(b, n_heads),
        in_specs=[
            pl.BlockSpec((1, s, _HEAD_DIM), lambda bi, h: (bi, 0, 2 * n_heads + h)),
            pl.BlockSpec((1, s, _HEAD_DIM), lambda bi, h: (bi, 0, h)),
            pl.BlockSpec((1, s, _HEAD_DIM), lambda bi, h: (bi, 0, n_heads + h)),
        ],
        out_specs=pl.BlockSpec((1, s, _HEAD_DIM), lambda bi, h: (bi, 0, h)),
        out_shape=jax.ShapeDtypeStruct((b, s, d), _BF16),
        scratch_shapes=[
            pltpu.VMEM((n_blocks, t, n_win * t), _F32),
            pltpu.VMEM((n_blocks * n_win * t, 2 * t), _BF16),
            pltpu.VMEM((n_blocks * n_win * t, 2 * t), _F32),
        ],
        compiler_params=pltpu.CompilerParams(
            dimension_semantics=("parallel", "arbitrary"),
            vmem_limit_bytes=_VMEM_LIMIT_BYTES),
    )(kvq, kvq, kvq)


def _oproj_kernel(x_ref, o_ref_in, w_ref, out_ref):
    out_ref[...] = x_ref[...] + jnp.dot(o_ref_in[...], w_ref[...], preferred_element_type=_F32)


def _out_proj(x2d, o2d, w_o, *, tm=512):
    m, d = x2d.shape
    return pl.pallas_call(
        _oproj_kernel,
        grid=(m // tm,),
        in_specs=[
            pl.BlockSpec((tm, d), lambda i: (i, 0)),
            pl.BlockSpec((tm, d), lambda i: (i, 0)),
            pl.BlockSpec((d, d), lambda i: (0, 0)),
        ],
        out_specs=pl.BlockSpec((tm, d), lambda i: (i, 0)),
        out_shape=jax.ShapeDtypeStruct((m, d), _F32),
        compiler_params=pltpu.CompilerParams(
            dimension_semantics=("parallel",),
            vmem_limit_bytes=_VMEM_LIMIT_BYTES),
    )(x2d, o2d, w_o)


def kernel(x, pool_norm, pool_w, pool_scale, kv_norm, w_kv, attn_norm, w_q, w_o, mlp_norm,
           w_up, w_down, final_norm):
    b, s, d = x.shape
    n_heads = d // _HEAD_DIM
    m = b * s
    assert pool_norm.shape[0] == 1 and attn_norm.shape[0] == 1 and mlp_norm.shape[0] == 2

    x1 = _pool_mixer(x, pool_norm[0], pool_w[0].astype(_BF16), pool_scale[0])
    x2 = _mlp(x1.reshape(m, d), mlp_norm[0], w_up[0].astype(_BF16), w_down[0].astype(_BF16))

    w_cat = jnp.concatenate([w_kv, w_q[0]], axis=1).astype(_BF16)
    q_scale = math.log2(math.e) / math.sqrt(_HEAD_DIM)
    kvq = _kvq_proj(x2, kv_norm, attn_norm[0], w_cat, 2 * d, q_scale)
    o = _stick_breaking_attention(kvq.reshape(b, s, 3 * d), b, s, d, n_heads)
    x3 = _out_proj(x2, o.reshape(m, d), w_o[0].astype(_BF16))

    out = _mlp(x3, mlp_norm[1], w_up[1].astype(_BF16), w_down[1].astype(_BF16), final_norm)
    return out.reshape(b, s, d)
```

```python
import functools
import math

import jax
import jax.numpy as jnp
from jax import lax
from jax.experimental import pallas as pl
from jax.experimental.pallas import tpu as pltpu

_EPS = 1e-6
_POOL_WINDOWS = (2, 4, 8, 16)
_HEAD_DIM = 128
_POOL_HALO = 16
_ATTN_BLOCK = 128
_ATTN_BACK_BLOCKS = 2
_LOG2_WEIGHT_FLOOR = -160.0
_VMEM_LIMIT_BYTES = 56 * 1024 * 1024

_F32 = jnp.float32
_BF16 = jnp.bfloat16


def _inv_rms(x):
    return lax.rsqrt(jnp.mean(x * x, axis=-1, keepdims=True) + _EPS)


def _pool_kernel(x_ref, halo_ref, g_ref, w_ref, sc_ref, o_ref, *, ts, group_dim):
    i = pl.program_id(1)
    g = g_ref[...]
    x = x_ref[0]
    xn = x * _inv_rms(x) * g
    h = halo_ref[0]
    hn = h * _inv_rms(h) * g
    hn = jnp.where(i > 0, hn, 0.0)
    ext = jnp.concatenate([hn, xn], axis=0)
    pos = i * ts + lax.broadcasted_iota(jnp.int32, (ts, 1), 0)
    for gi, w in enumerate(_POOL_WINDOWS):
        c0 = gi * group_dim
        s = ext[:, c0:c0 + group_dim]
        k = 1
        while k < w:
            s = s + pltpu.roll(s, k, axis=0)
            k *= 2
        win = s[_POOL_HALO:]
        inv_cnt = 1.0 / jnp.minimum(pos + 1, w).astype(_F32)
        diff = win * inv_cnt - xn[:, c0:c0 + group_dim]
        y = jnp.dot(diff.astype(_BF16), w_ref[gi], preferred_element_type=_F32)
        o_ref[0, :, c0:c0 + group_dim] = x[:, c0:c0 + group_dim] + y * sc_ref[:, c0:c0 + group_dim]


def _pool_mixer(x, norm_g, w_bf16, scale, *, ts=512):
    b, s, d = x.shape
    n_groups = len(_POOL_WINDOWS)
    group_dim = d // n_groups
    halo_blocks_per_tile = ts // _POOL_HALO
    kern = functools.partial(_pool_kernel, ts=ts, group_dim=group_dim)
    return pl.pallas_call(
        kern,
        grid=(b, s // ts),
        in_specs=[
            pl.BlockSpec((1, ts, d), lambda bi, i: (bi, i, 0)),
            pl.BlockSpec((1, _POOL_HALO, d),
                         lambda bi, i: (bi, jnp.maximum(i * halo_blocks_per_tile - 1, 0), 0)),
            pl.BlockSpec((1, d), lambda bi, i: (0, 0)),
            pl.BlockSpec((n_groups, group_dim, group_dim), lambda bi, i: (0, 0, 0)),
            pl.BlockSpec((1, d), lambda bi, i: (0, 0)),
        ],
        out_specs=pl.BlockSpec((1, ts, d), lambda bi, i: (bi, i, 0)),
        out_shape=jax.ShapeDtypeStruct((b, s, d), _F32),
        compiler_params=pltpu.CompilerParams(
            dimension_semantics=("parallel", "arbitrary"),
            vmem_limit_bytes=_VMEM_LIMIT_BYTES),
    )(x, x, norm_g.reshape(1, d), w_bf16, scale.reshape(1, d))


def _mlp_kernel(x_ref, g_ref, wu_ref, wd_ref, *rest, final_norm):
    if final_norm:
        gf_ref, o_ref, xn_ref = rest
    else:
        o_ref, xn_ref = rest
    f = pl.program_id(1)

    @pl.when(f == 0)
    def _():
        x = x_ref[...]
        xn_ref[...] = (x * _inv_rms(x) * g_ref[...]).astype(_BF16)
        o_ref[...] = x

    h = jnp.dot(xn_ref[...], wu_ref[...], preferred_element_type=_F32)
    h = jnp.maximum(h, 0.0)
    h = (h * h).astype(_BF16)
    o_ref[...] += jnp.dot(h, wd_ref[...], preferred_element_type=_F32)

    if final_norm:
        @pl.when(f == pl.num_programs(1) - 1)
        def _():
            r = o_ref[...]
            o_ref[...] = r * _inv_rms(r) * gf_ref[...]


def _mlp(x2d, norm_g, w_up, w_down, final_g=None, *, tm=1024, tf=512):
    m, d = x2d.shape
    d_ff = w_up.shape[1]
    final_norm = final_g is not None
    in_specs = [
        pl.BlockSpec((tm, d), lambda i, f: (i, 0)),
        pl.BlockSpec((1, d), lambda i, f: (0, 0)),
        pl.BlockSpec((d, tf), lambda i, f: (0, f)),
        pl.BlockSpec((tf, d), lambda i, f: (f, 0)),
    ]
    args = [x2d, norm_g.reshape(1, d), w_up, w_down]
    if final_norm:
        in_specs.append(pl.BlockSpec((1, d), lambda i, f: (0, 0)))
        args.append(final_g.reshape(1, d))
    return pl.pallas_call(
        functools.partial(_mlp_kernel, final_norm=final_norm),
        grid=(m // tm, d_ff // tf),
        in_specs=in_specs,
        out_specs=pl.BlockSpec((tm, d), lambda i, f: (i, 0)),
        out_shape=jax.ShapeDtypeStruct((m, d), _F32),
        scratch_shapes=[pltpu.VMEM((tm, d), _BF16)],
        compiler_params=pltpu.CompilerParams(
            dimension_semantics=("parallel", "arbitrary"),
            vmem_limit_bytes=_VMEM_LIMIT_BYTES),
    )(*args)


def _proj_kernel(x_ref, g_ref, w_ref, o_ref, *, q_part, q_scale):
    x = x_ref[...]
    xn = (x * _inv_rms(x) * g_ref[0]).astype(_BF16)
    y = jnp.dot(xn, w_ref[...], preferred_element_type=_F32)
    y = y * jnp.where(pl.program_id(1) == q_part, q_scale, 1.0)
    for h in range(o_ref.shape[0]):
        o_ref[h] = y[:, h * _HEAD_DIM:(h + 1) * _HEAD_DIM].astype(_BF16)


def _kvq_proj(x2d, gains, w_cat, q_scale, n_heads, *, tm=1024):
    m, d = x2d.shape
    n_parts = gains.shape[0]
    assert w_cat.shape == (d, n_parts * d) and d == n_heads * _HEAD_DIM
    return pl.pallas_call(
        functools.partial(_proj_kernel, q_part=n_parts - 1, q_scale=q_scale),
        grid=(m // tm, n_parts),
        in_specs=[
            pl.BlockSpec((tm, d), lambda i, n: (i, 0)),
            pl.BlockSpec((1, 1, d), lambda i, n: (n, 0, 0)),
            pl.BlockSpec((d, d), lambda i, n: (0, n)),
        ],
        out_specs=pl.BlockSpec((n_heads, tm, _HEAD_DIM), lambda i, n: (n, i, 0)),
        out_shape=jax.ShapeDtypeStruct((n_parts * n_heads, m, _HEAD_DIM), _BF16),
        compiler_params=pltpu.CompilerParams(
            dimension_semantics=("parallel", "arbitrary"),
            vmem_limit_bytes=_VMEM_LIMIT_BYTES),
    )(x2d, gains, w_cat)


def _log2_sigmoid(z2):
    return jnp.minimum(z2, 0.0) - jnp.log2(1.0 + jnp.exp2(-jnp.abs(z2)))


def _split_hi_lo(x):
    hi = x.astype(_BF16)
    lo = (x - hi.astype(_F32)).astype(_BF16)
    return jnp.concatenate([hi, lo], axis=1)


def _attn_kernel(q_ref, k_ref, v_ref, o_ref, lb_ref, lhs_ref, cs_ref, *, seq):
    t = _ATTN_BLOCK
    n_blocks = seq // t
    row = lax.broadcasted_iota(jnp.int32, (t, t), 0)
    col = lax.broadcasted_iota(jnp.int32, (t, t), 1)
    strictly_causal = col < row
    later = jnp.where(row > col, 1.0, 0.0).astype(_BF16)
    ones = jnp.ones((t, t), _BF16)
    sums = jnp.concatenate([jnp.concatenate([later, ones], axis=1)] * 2, axis=0)
    nt_dims = (((1,), (1,)), ((), ()))

    n_win = _ATTN_BACK_BLOCKS + 1

    def logs_phase(i, q0, n_back):
        w0 = q0 - n_back * t
        z_all = lax.dot_general(q_ref[0, pl.ds(q0, t), :], k_ref[0, pl.ds(w0, (n_back + 1) * t), :],
                                nt_dims, preferred_element_type=_F32)
        for jb in range(n_back + 1):
            z2 = z_all[:, (n_back - jb) * t:(n_back - jb + 1) * t]
            lb = _log2_sigmoid(z2)
            l1 = lb - z2
            if jb == 0:
                l1 = jnp.where(strictly_causal, l1, 0.0)
            lb_ref[i, :, jb * t:(jb + 1) * t] = lb
            lhs_ref[pl.ds((i * n_win + jb) * t, t), :] = _split_hi_lo(l1)
        for jb in range(n_back + 1, n_win):
            lhs_ref[pl.ds((i * n_win + jb) * t, t), :] = jnp.zeros((t, 2 * t), _BF16)

    def weights_phase(i, q0, n_back):
        w0 = q0 - n_back * t
        carry = None
        weights = []
        for jb in range(n_back + 1):
            rows = pl.ds((i * n_win + jb) * t, t)
            between = cs_ref[rows, :t] if carry is None else cs_ref[rows, :t] + carry
            a = jnp.exp2(lb_ref[i, :, jb * t:(jb + 1) * t] + between)
            if jb == 0:
                a = jnp.where(strictly_causal, a, 0.0)
            weights.append(a.astype(_BF16))
            carry = cs_ref[rows, t:] if carry is None else carry + cs_ref[rows, t:]
        a_all = jnp.concatenate(weights[::-1], axis=1)
        out = jnp.dot(a_all, v_ref[0, pl.ds(w0, (n_back + 1) * t), :], preferred_element_type=_F32)
        o_ref[0, pl.ds(q0, t), :] = out.astype(o_ref.dtype)
        return carry

    def run_phase(phase, init, unroll):
        for i in range(_ATTN_BACK_BLOCKS):
            phase(i, i * t, i)

        def step(i, acc):
            r = phase(i, pl.multiple_of(i * t, t), _ATTN_BACK_BLOCKS)
            return acc if r is None else jnp.maximum(acc, r)

        return lax.fori_loop(_ATTN_BACK_BLOCKS, n_blocks, step, init, unroll=unroll)

    run_phase(logs_phase, 0, unroll=5)
    cs_ref[...] = jnp.dot(lhs_ref[...], sums, preferred_element_type=_F32)
    worst = run_phase(weights_phase, jnp.full((t, t), -jnp.inf, _F32), unroll=5)

    @pl.when(jnp.max(worst) > _LOG2_WEIGHT_FLOOR)
    def _():
        def tile(q, k0, carry, acc, diag):
            z2 = lax.dot_general(q, k_ref[0, pl.ds(k0, t), :], nt_dims, preferred_element_type=_F32)
            lb = _log2_sigmoid(z2)
            l1 = lb - z2
            if diag:
                l1 = jnp.where(strictly_causal, l1, 0.0)
            cs = jnp.dot(_split_hi_lo(l1), sums, preferred_element_type=_F32)
            a = jnp.exp2(lb + cs[:, :t] + carry)
            if diag:
                a = jnp.where(strictly_causal, a, 0.0)
            acc = acc + jnp.dot(a.astype(_BF16), v_ref[0, pl.ds(k0, t), :], preferred_element_type=_F32)
            return carry + cs[:, t:], acc

        def q_block(i, _):
            q0 = pl.multiple_of(i * t, t)
            q = q_ref[0, pl.ds(q0, t), :]
            carry, acc = tile(q, q0, jnp.zeros((t, t), _F32), jnp.zeros((t, _HEAD_DIM), _F32), True)

            def cond(state):
                j, cmax, _, _ = state
                return jnp.logical_and(j >= 0, cmax > _LOG2_WEIGHT_FLOOR)

            def body(state):
                j, _, carry, acc = state
                carry, acc = tile(q, pl.multiple_of(j * t, t), carry, acc, False)
                return j - 1, jnp.max(carry), carry, acc

            _, _, _, acc = lax.while_loop(cond, body, (i - 1, jnp.max(carry), carry, acc))
            o_ref[0, pl.ds(q0, t), :] = acc.astype(o_ref.dtype)
            return 0

        lax.fori_loop(0, n_blocks, q_block, 0)


def _stick_breaking_attention(kvq, b, s, n_heads):
    t, n_win = _ATTN_BLOCK, _ATTN_BACK_BLOCKS + 1
    n_blocks = s // t
    assert s % t == 0 and n_blocks >= n_win
    return pl.pallas_call(
        functools.partial(_attn_kernel, seq=s),
        grid=(b, n_heads),
        in_specs=[
            pl.BlockSpec((1, s, _HEAD_DIM), lambda bi, h: (2 * n_heads + h, bi, 0)),
            pl.BlockSpec((1, s, _HEAD_DIM), lambda bi, h: (h, bi, 0)),
            pl.BlockSpec((1, s, _HEAD_DIM), lambda bi, h: (n_heads + h, bi, 0)),
        ],
        out_specs=pl.BlockSpec((1, s, _HEAD_DIM), lambda bi, h: (h, bi, 0)),
        out_shape=jax.ShapeDtypeStruct((n_heads, b * s, _HEAD_DIM), _BF16),
        scratch_shapes=[
            pltpu.VMEM((n_blocks, t, n_win * t), _F32),
            pltpu.VMEM((n_blocks * n_win * t, 2 * t), _BF16),
            pltpu.VMEM((n_blocks * n_win * t, 2 * t), _F32),
        ],
        compiler_params=pltpu.CompilerParams(
            dimension_semantics=("parallel", "arbitrary"),
            vmem_limit_bytes=_VMEM_LIMIT_BYTES),
    )(kvq, kvq, kvq)


def _oproj_kernel(x_ref, o_ref_in, w_ref, out_ref):
    o = jnp.concatenate([o_ref_in[h] for h in range(o_ref_in.shape[0])], axis=1)
    out_ref[...] = x_ref[...] + jnp.dot(o, w_ref[...], preferred_element_type=_F32)


def _out_proj(x2d, o_heads, w_o, *, tm=512):
    m, d = x2d.shape
    n_heads = o_heads.shape[0]
    return pl.pallas_call(
        _oproj_kernel,
        grid=(m // tm,),
        in_specs=[
            pl.BlockSpec((tm, d), lambda i: (i, 0)),
            pl.BlockSpec((n_heads, tm, _HEAD_DIM), lambda i: (0, i, 0)),
            pl.BlockSpec((d, d), lambda i: (0, 0)),
        ],
        out_specs=pl.BlockSpec((tm, d), lambda i: (i, 0)),
        out_shape=jax.ShapeDtypeStruct((m, d), _F32),
        compiler_params=pltpu.CompilerParams(
            dimension_semantics=("parallel",),
            vmem_limit_bytes=_VMEM_LIMIT_BYTES),
    )(x2d, o_heads, w_o)


def kernel(x, pool_norm, pool_w, pool_scale, kv_norm, w_kv, attn_norm, w_q, w_o, mlp_norm,
           w_up, w_down, final_norm):
    b, s, d = x.shape
    n_heads = d // _HEAD_DIM
    m = b * s
    assert pool_norm.shape[0] == 1 and attn_norm.shape[0] == 1 and mlp_norm.shape[0] == 2

    x1 = _pool_mixer(x, pool_norm[0], pool_w[0].astype(_BF16), pool_scale[0])
    x2 = _mlp(x1.reshape(m, d), mlp_norm[0], w_up[0].astype(_BF16), w_down[0].astype(_BF16))

    w_cat = jnp.concatenate([w_kv, w_q[0]], axis=1).astype(_BF16)
    q_scale = math.log2(math.e) / math.sqrt(_HEAD_DIM)
    gains = jnp.stack([kv_norm, kv_norm, attn_norm[0]]).reshape(3, 1, d)
    kvq = _kvq_proj(x2, gains, w_cat, q_scale, n_heads)
    o = _stick_breaking_attention(kvq, b, s, n_heads)
    x3 = _out_proj(x2, o, w_o[0].astype(_BF16))

    out = _mlp(x3, mlp_norm[1], w_up[1].astype(_BF16), w_down[1].astype(_BF16), final_norm)
    return out.reshape(b, s, d)
```

```python
import functools
import math

import jax
import jax.numpy as jnp
from jax import lax
from jax.experimental import pallas as pl
from jax.experimental.pallas import tpu as pltpu

_EPS = 1e-6
_POOL_WINDOWS = (2, 4, 8, 16)
_HEAD_DIM = 128
_POOL_HALO = 16
_ATTN_BLOCK = 128
_ATTN_BACK_BLOCKS = 2
_LOG2_WEIGHT_FLOOR = -150.0
_VMEM_LIMIT_BYTES = 56 * 1024 * 1024

_F32 = jnp.float32
_BF16 = jnp.bfloat16


def _inv_rms(x):
    return lax.rsqrt(jnp.mean(x * x, axis=-1, keepdims=True) + _EPS)


def _pool_kernel(x_ref, halo_ref, g_ref, w_ref, sc_ref, o_ref, *, ts, group_dim):
    i = pl.program_id(1)
    g = g_ref[...]
    x = x_ref[0]
    xn = x * _inv_rms(x) * g
    h = halo_ref[0]
    hn = h * _inv_rms(h) * g
    hn = jnp.where(i > 0, hn, 0.0)
    ext = jnp.concatenate([hn, xn], axis=0)
    pos = i * ts + lax.broadcasted_iota(jnp.int32, (ts, 1), 0)
    for gi, w in enumerate(_POOL_WINDOWS):
        c0 = gi * group_dim
        s = ext[:, c0:c0 + group_dim]
        k = 1
        while k < w:
            s = s + pltpu.roll(s, k, axis=0)
            k *= 2
        win = s[_POOL_HALO:]
        inv_cnt = 1.0 / jnp.minimum(pos + 1, w).astype(_F32)
        diff = win * inv_cnt - xn[:, c0:c0 + group_dim]
        y = jnp.dot(diff.astype(_BF16), w_ref[gi], preferred_element_type=_F32)
        o_ref[0, :, c0:c0 + group_dim] = x[:, c0:c0 + group_dim] + y * sc_ref[:, c0:c0 + group_dim]


def _pool_mixer(x, norm_g, w_bf16, scale, *, ts=512):
    b, s, d = x.shape
    n_groups = len(_POOL_WINDOWS)
    group_dim = d // n_groups
    halo_blocks_per_tile = ts // _POOL_HALO
    kern = functools.partial(_pool_kernel, ts=ts, group_dim=group_dim)
    return pl.pallas_call(
        kern,
        grid=(b, s // ts),
        in_specs=[
            pl.BlockSpec((1, ts, d), lambda bi, i: (bi, i, 0)),
            pl.BlockSpec((1, _POOL_HALO, d),
                         lambda bi, i: (bi, jnp.maximum(i * halo_blocks_per_tile - 1, 0), 0)),
            pl.BlockSpec((1, d), lambda bi, i: (0, 0)),
            pl.BlockSpec((n_groups, group_dim, group_dim), lambda bi, i: (0, 0, 0)),
            pl.BlockSpec((1, d), lambda bi, i: (0, 0)),
        ],
        out_specs=pl.BlockSpec((1, ts, d), lambda bi, i: (bi, i, 0)),
        out_shape=jax.ShapeDtypeStruct((b, s, d), _F32),
        compiler_params=pltpu.CompilerParams(
            dimension_semantics=("parallel", "arbitrary"),
            vmem_limit_bytes=_VMEM_LIMIT_BYTES),
    )(x, x, norm_g.reshape(1, d), w_bf16, scale.reshape(1, d))


def _mlp_kernel(x_ref, g_ref, wu_ref, wd_ref, *rest, final_norm):
    if final_norm:
        gf_ref, o_ref, xn_ref = rest
    else:
        o_ref, xn_ref = rest
    f = pl.program_id(1)

    @pl.when(f == 0)
    def _():
        x = x_ref[...]
        xn_ref[...] = (x * _inv_rms(x) * g_ref[...]).astype(_BF16)
        o_ref[...] = x

    h = jnp.dot(xn_ref[...], wu_ref[...], preferred_element_type=_F32)
    h = jnp.maximum(h, 0.0)
    h = (h * h).astype(_BF16)
    o_ref[...] += jnp.dot(h, wd_ref[...], preferred_element_type=_F32)

    if final_norm:
        @pl.when(f == pl.num_programs(1) - 1)
        def _():
            r = o_ref[...]
            o_ref[...] = r * _inv_rms(r) * gf_ref[...]


def _mlp(x2d, norm_g, w_up, w_down, final_g=None, *, tm=1024, tf=512):
    m, d = x2d.shape
    d_ff = w_up.shape[1]
    final_norm = final_g is not None
    in_specs = [
        pl.BlockSpec((tm, d), lambda i, f: (i, 0)),
        pl.BlockSpec((1, d), lambda i, f: (0, 0)),
        pl.BlockSpec((d, tf), lambda i, f: (0, f)),
        pl.BlockSpec((tf, d), lambda i, f: (f, 0)),
    ]
    args = [x2d, norm_g.reshape(1, d), w_up, w_down]
    if final_norm:
        in_specs.append(pl.BlockSpec((1, d), lambda i, f: (0, 0)))
        args.append(final_g.reshape(1, d))
    return pl.pallas_call(
        functools.partial(_mlp_kernel, final_norm=final_norm),
        grid=(m // tm, d_ff // tf),
        in_specs=in_specs,
        out_specs=pl.BlockSpec((tm, d), lambda i, f: (i, 0)),
        out_shape=jax.ShapeDtypeStruct((m, d), _F32),
        scratch_shapes=[pltpu.VMEM((tm, d), _BF16)],
        compiler_params=pltpu.CompilerParams(
            dimension_semantics=("parallel", "arbitrary"),
            vmem_limit_bytes=_VMEM_LIMIT_BYTES),
    )(*args)


def _proj_kernel(x_ref, g_ref, w_ref, o_ref, *, q_part, q_scale):
    x = x_ref[...]
    xn = (x * _inv_rms(x) * g_ref[0]).astype(_BF16)
    y = jnp.dot(xn, w_ref[...], preferred_element_type=_F32)
    y = y * jnp.where(pl.program_id(1) == q_part, q_scale, 1.0)
    for h in range(o_ref.shape[0]):
        o_ref[h] = y[:, h * _HEAD_DIM:(h + 1) * _HEAD_DIM].astype(_BF16)


def _kvq_proj(x2d, gains, w_cat, q_scale, n_heads, *, tm=1024):
    m, d = x2d.shape
    n_parts = gains.shape[0]
    assert w_cat.shape == (d, n_parts * d) and d == n_heads * _HEAD_DIM
    return pl.pallas_call(
        functools.partial(_proj_kernel, q_part=n_parts - 1, q_scale=q_scale),
        grid=(m // tm, n_parts),
        in_specs=[
            pl.BlockSpec((tm, d), lambda i, n: (i, 0)),
            pl.BlockSpec((1, 1, d), lambda i, n: (n, 0, 0)),
            pl.BlockSpec((d, d), lambda i, n: (0, n)),
        ],
        out_specs=pl.BlockSpec((n_heads, tm, _HEAD_DIM), lambda i, n: (n, i, 0)),
        out_shape=jax.ShapeDtypeStruct((n_parts * n_heads, m, _HEAD_DIM), _BF16),
        compiler_params=pltpu.CompilerParams(
            dimension_semantics=("parallel", "arbitrary"),
            vmem_limit_bytes=_VMEM_LIMIT_BYTES),
    )(x2d, gains, w_cat)


def _log2_sigmoid(z2):
    return jnp.minimum(z2, 0.0) - jnp.log2(1.0 + jnp.exp2(-jnp.abs(z2)))


def _split_hi_lo(x):
    hi = x.astype(_BF16)
    lo = (x - hi.astype(_F32)).astype(_BF16)
    return jnp.concatenate([hi, lo], axis=1)


def _attn_kernel(q_ref, k_ref, v_ref, o_ref, lb_ref, lhs_ref, cs_ref, reach_ref, *, seq):
    t = _ATTN_BLOCK
    n_blocks = seq // t
    row = lax.broadcasted_iota(jnp.int32, (t, t), 0)
    col = lax.broadcasted_iota(jnp.int32, (t, t), 1)
    strictly_causal = col < row
    later = jnp.where(row > col, 1.0, 0.0).astype(_BF16)
    ones = jnp.ones((t, t), _BF16)
    sums = jnp.concatenate([jnp.concatenate([later, ones], axis=1)] * 2, axis=0)
    nt_dims = (((1,), (1,)), ((), ()))

    n_win = _ATTN_BACK_BLOCKS + 1

    def logs_phase(i, q0, n_back):
        w0 = q0 - n_back * t
        z_all = lax.dot_general(q_ref[0, pl.ds(q0, t), :], k_ref[0, pl.ds(w0, (n_back + 1) * t), :],
                                nt_dims, preferred_element_type=_F32)
        for jb in range(n_back + 1):
            z2 = z_all[:, (n_back - jb) * t:(n_back - jb + 1) * t]
            lb = _log2_sigmoid(z2)
            l1 = lb - z2
            if jb == 0:
                l1 = jnp.where(strictly_causal, l1, 0.0)
            lb_ref[i, :, jb * t:(jb + 1) * t] = lb
            lhs_ref[pl.ds((i * n_win + jb) * t, t), :] = _split_hi_lo(l1)
        for jb in range(n_back + 1, n_win):
            lhs_ref[pl.ds((i * n_win + jb) * t, t), :] = jnp.zeros((t, 2 * t), _BF16)

    def weights_phase(i, q0, n_back):
        w0 = q0 - n_back * t
        carry = None
        weights = []
        for jb in range(n_back + 1):
            rows = pl.ds((i * n_win + jb) * t, t)
            between = cs_ref[rows, :t] if carry is None else cs_ref[rows, :t] + carry
            a = jnp.exp2(lb_ref[i, :, jb * t:(jb + 1) * t] + between)
            if jb == 0:
                a = jnp.where(strictly_causal, a, 0.0)
            weights.append(a.astype(_BF16))
            carry = cs_ref[rows, t:] if carry is None else carry + cs_ref[rows, t:]
        a_all = jnp.concatenate(weights[::-1], axis=1)
        out = jnp.dot(a_all, v_ref[0, pl.ds(w0, (n_back + 1) * t), :], preferred_element_type=_F32)
        o_ref[0, pl.ds(q0, t), :] = out.astype(o_ref.dtype)
        reach_ref[i] = jnp.max(carry)

    def run_phase(phase, unroll):
        for i in range(_ATTN_BACK_BLOCKS):
            phase(i, i * t, i)

        def step(i, _):
            phase(i, pl.multiple_of(i * t, t), _ATTN_BACK_BLOCKS)
            return 0

        lax.fori_loop(_ATTN_BACK_BLOCKS, n_blocks, step, 0, unroll=unroll)

    run_phase(logs_phase, unroll=5)
    cs_ref[...] = jnp.dot(lhs_ref[...], sums, preferred_element_type=_F32)
    run_phase(weights_phase, unroll=5)

    def continue_block(i, _):
        @pl.when(reach_ref[i] > _LOG2_WEIGHT_FLOOR)
        def _():
            q0 = pl.multiple_of(i * t, t)
            q = q_ref[0, pl.ds(q0, t), :]
            carry = cs_ref[pl.ds(i * n_win * t, t), t:]
            for jb in range(1, n_win):
                carry = carry + cs_ref[pl.ds((i * n_win + jb) * t, t), t:]

            def cond(state):
                j, reach, _, _ = state
                return jnp.logical_and(j >= 0, reach > _LOG2_WEIGHT_FLOOR)

            def body(state):
                j, _, carry, acc = state
                k0 = pl.multiple_of(j * t, t)
                z2 = lax.dot_general(q, k_ref[0, pl.ds(k0, t), :], nt_dims, preferred_element_type=_F32)
                lb = _log2_sigmoid(z2)
                cs = jnp.dot(_split_hi_lo(lb - z2), sums, preferred_element_type=_F32)
                a = jnp.exp2(lb + cs[:, :t] + carry)
                acc = acc + jnp.dot(a.astype(_BF16), v_ref[0, pl.ds(k0, t), :],
                                    preferred_element_type=_F32)
                carry = carry + cs[:, t:]
                return j - 1, jnp.max(carry), carry, acc

            acc = o_ref[0, pl.ds(q0, t), :].astype(_F32)
            _, _, _, acc = lax.while_loop(cond, body, (i - n_win, reach_ref[i], carry, acc))
            o_ref[0, pl.ds(q0, t), :] = acc.astype(o_ref.dtype)

        return 0

    lax.fori_loop(n_win, n_blocks, continue_block, 0)


def _stick_breaking_attention(kvq, b, s, n_heads):
    t, n_win = _ATTN_BLOCK, _ATTN_BACK_BLOCKS + 1
    n_blocks = s // t
    assert s % t == 0 and n_blocks >= n_win
    return pl.pallas_call(
        functools.partial(_attn_kernel, seq=s),
        grid=(b, n_heads),
        in_specs=[
            pl.BlockSpec((1, s, _HEAD_DIM), lambda bi, h: (2 * n_heads + h, bi, 0)),
            pl.BlockSpec((1, s, _HEAD_DIM), lambda bi, h: (h, bi, 0)),
            pl.BlockSpec((1, s, _HEAD_DIM), lambda bi, h: (n_heads + h, bi, 0)),
        ],
        out_specs=pl.BlockSpec((1, s, _HEAD_DIM), lambda bi, h: (h, bi, 0)),
        out_shape=jax.ShapeDtypeStruct((n_heads, b * s, _HEAD_DIM), _BF16),
        scratch_shapes=[
            pltpu.VMEM((n_blocks, t, n_win * t), _F32),
            pltpu.VMEM((n_blocks * n_win * t, 2 * t), _BF16),
            pltpu.VMEM((n_blocks * n_win * t, 2 * t), _F32),
            pltpu.SMEM((n_blocks,), _F32),
        ],
        compiler_params=pltpu.CompilerParams(
            dimension_semantics=("parallel", "arbitrary"),
            vmem_limit_bytes=_VMEM_LIMIT_BYTES),
    )(kvq, kvq, kvq)


def _oproj_kernel(x_ref, o_ref_in, w_ref, out_ref):
    o = jnp.concatenate([o_ref_in[h] for h in range(o_ref_in.shape[0])], axis=1)
    out_ref[...] = x_ref[...] + jnp.dot(o, w_ref[...], preferred_element_type=_F32)


def _out_proj(x2d, o_heads, w_o, *, tm=512):
    m, d = x2d.shape
    n_heads = o_heads.shape[0]
    return pl.pallas_call(
        _oproj_kernel,
        grid=(m // tm,),
        in_specs=[
            pl.BlockSpec((tm, d), lambda i: (i, 0)),
            pl.BlockSpec((n_heads, tm, _HEAD_DIM), lambda i: (0, i, 0)),
            pl.BlockSpec((d, d), lambda i: (0, 0)),
        ],
        out_specs=pl.BlockSpec((tm, d), lambda i: (i, 0)),
        out_shape=jax.ShapeDtypeStruct((m, d), _F32),
        compiler_params=pltpu.CompilerParams(
            dimension_semantics=("parallel",),
            vmem_limit_bytes=_VMEM_LIMIT_BYTES),
    )(x2d, o_heads, w_o)


def kernel(x, pool_norm, pool_w, pool_scale, kv_norm, w_kv, attn_norm, w_q, w_o, mlp_norm,
           w_up, w_down, final_norm):
    b, s, d = x.shape
    n_heads = d // _HEAD_DIM
    m = b * s
    assert pool_norm.shape[0] == 1 and attn_norm.shape[0] == 1 and mlp_norm.shape[0] == 2

    x1 = _pool_mixer(x, pool_norm[0], pool_w[0].astype(_BF16), pool_scale[0])
    x2 = _mlp(x1.reshape(m, d), mlp_norm[0], w_up[0].astype(_BF16), w_down[0].astype(_BF16))

    w_cat = jnp.concatenate([w_kv, w_q[0]], axis=1).astype(_BF16)
    q_scale = math.log2(math.e) / math.sqrt(_HEAD_DIM)
    gains = jnp.stack([kv_norm, kv_norm, attn_norm[0]]).reshape(3, 1, d)
    kvq = _kvq_proj(x2, gains, w_cat, q_scale, n_heads)
    o = _stick_breaking_attention(kvq, b, s, n_heads)
    x3 = _out_proj(x2, o, w_o[0].astype(_BF16))

    out = _mlp(x3, mlp_norm[1], w_up[1].astype(_BF16), w_down[1].astype(_BF16), final_norm)
    return out.reshape(b, s, d)
```

```python
import functools
import math

import jax
import jax.numpy as jnp
from jax import lax
from jax.experimental import pallas as pl
from jax.experimental.pallas import tpu as pltpu

_EPS = 1e-6
_POOL_WINDOWS = (2, 4, 8, 16)
_HEAD_DIM = 128
_POOL_HALO = 16
_ATTN_BLOCK = 128
_ATTN_BACK_BLOCKS = 2
_LOG2_WEIGHT_FLOOR = -150.0
_VMEM_LIMIT_BYTES = 56 * 1024 * 1024

_F32 = jnp.float32
_BF16 = jnp.bfloat16


def _inv_rms(x):
    return lax.rsqrt(jnp.mean(x * x, axis=-1, keepdims=True) + _EPS)


def _pool_kernel(x_ref, halo_ref, g_ref, w_ref, sc_ref, o_ref, *, ts, group_dim):
    i = pl.program_id(1)
    g = g_ref[...]
    x = x_ref[0]
    xn = x * _inv_rms(x) * g
    h = halo_ref[0]
    hn = h * _inv_rms(h) * g
    hn = jnp.where(i > 0, hn, 0.0)
    ext = jnp.concatenate([hn, xn], axis=0)
    pos = i * ts + lax.broadcasted_iota(jnp.int32, (ts, 1), 0)
    for gi, w in enumerate(_POOL_WINDOWS):
        c0 = gi * group_dim
        s = ext[:, c0:c0 + group_dim]
        k = 1
        while k < w:
            s = s + pltpu.roll(s, k, axis=0)
            k *= 2
        win = s[_POOL_HALO:]
        inv_cnt = 1.0 / jnp.minimum(pos + 1, w).astype(_F32)
        diff = win * inv_cnt - xn[:, c0:c0 + group_dim]
        y = jnp.dot(diff.astype(_BF16), w_ref[gi], preferred_element_type=_F32)
        o_ref[0, :, c0:c0 + group_dim] = x[:, c0:c0 + group_dim] + y * sc_ref[:, c0:c0 + group_dim]


def _pool_mixer(x, norm_g, w_bf16, scale, *, ts=512):
    b, s, d = x.shape
    n_groups = len(_POOL_WINDOWS)
    group_dim = d // n_groups
    halo_blocks_per_tile = ts // _POOL_HALO
    kern = functools.partial(_pool_kernel, ts=ts, group_dim=group_dim)
    return pl.pallas_call(
        kern,
        grid=(b, s // ts),
        in_specs=[
            pl.BlockSpec((1, ts, d), lambda bi, i: (bi, i, 0)),
            pl.BlockSpec((1, _POOL_HALO, d),
                         lambda bi, i: (bi, jnp.maximum(i * halo_blocks_per_tile - 1, 0), 0)),
            pl.BlockSpec((1, d), lambda bi, i: (0, 0)),
            pl.BlockSpec((n_groups, group_dim, group_dim), lambda bi, i: (0, 0, 0)),
            pl.BlockSpec((1, d), lambda bi, i: (0, 0)),
        ],
        out_specs=pl.BlockSpec((1, ts, d), lambda bi, i: (bi, i, 0)),
        out_shape=jax.ShapeDtypeStruct((b, s, d), _F32),
        compiler_params=pltpu.CompilerParams(
            dimension_semantics=("parallel", "arbitrary"),
            vmem_limit_bytes=_VMEM_LIMIT_BYTES),
    )(x, x, norm_g.reshape(1, d), w_bf16, scale.reshape(1, d))


def _mlp_kernel(x_ref, g_ref, wu_ref, wd_ref, *rest, final_norm):
    if final_norm:
        gf_ref, o_ref, xn_ref = rest
    else:
        o_ref, xn_ref = rest
    f = pl.program_id(1)

    @pl.when(f == 0)
    def _():
        x = x_ref[...]
        xn_ref[...] = (x * _inv_rms(x) * g_ref[...]).astype(_BF16)
        o_ref[...] = x

    h = jnp.dot(xn_ref[...], wu_ref[0].astype(_BF16), preferred_element_type=_F32)
    h = jnp.maximum(h, 0.0)
    h = (h * h).astype(_BF16)
    o_ref[...] += jnp.dot(h, wd_ref[0].astype(_BF16), preferred_element_type=_F32)

    if final_norm:
        @pl.when(f == pl.num_programs(1) - 1)
        def _():
            r = o_ref[...]
            o_ref[...] = r * _inv_rms(r) * gf_ref[...]


def _mlp(x2d, norm_g, w_up, w_down, layer, final_g=None, *, tm=1024, tf=512):
    m, d = x2d.shape
    d_ff = w_up.shape[2]
    final_norm = final_g is not None
    in_specs = [
        pl.BlockSpec((tm, d), lambda i, f: (i, 0)),
        pl.BlockSpec((1, d), lambda i, f: (0, 0)),
        pl.BlockSpec((1, d, tf), lambda i, f: (layer, 0, f)),
        pl.BlockSpec((1, tf, d), lambda i, f: (layer, f, 0)),
    ]
    args = [x2d, norm_g.reshape(1, d), w_up, w_down]
    if final_norm:
        in_specs.append(pl.BlockSpec((1, d), lambda i, f: (0, 0)))
        args.append(final_g.reshape(1, d))
    return pl.pallas_call(
        functools.partial(_mlp_kernel, final_norm=final_norm),
        grid=(m // tm, d_ff // tf),
        in_specs=in_specs,
        out_specs=pl.BlockSpec((tm, d), lambda i, f: (i, 0)),
        out_shape=jax.ShapeDtypeStruct((m, d), _F32),
        scratch_shapes=[pltpu.VMEM((tm, d), _BF16)],
        compiler_params=pltpu.CompilerParams(
            dimension_semantics=("parallel", "arbitrary"),
            vmem_limit_bytes=_VMEM_LIMIT_BYTES),
    )(*args)


def _proj_kernel(x_ref, g_ref, w_ref, o_ref, *, q_part, q_scale):
    x = x_ref[...]
    xn = (x * _inv_rms(x) * g_ref[0]).astype(_BF16)
    y = jnp.dot(xn, w_ref[...], preferred_element_type=_F32)
    y = y * jnp.where(pl.program_id(1) == q_part, q_scale, 1.0)
    for h in range(o_ref.shape[0]):
        o_ref[h] = y[:, h * _HEAD_DIM:(h + 1) * _HEAD_DIM].astype(_BF16)


def _kvq_proj(x2d, gains, w_cat, q_scale, n_heads, *, tm=1024):
    m, d = x2d.shape
    n_parts = gains.shape[0]
    assert w_cat.shape == (d, n_parts * d) and d == n_heads * _HEAD_DIM
    return pl.pallas_call(
        functools.partial(_proj_kernel, q_part=n_parts - 1, q_scale=q_scale),
        grid=(m // tm, n_parts),
        in_specs=[
            pl.BlockSpec((tm, d), lambda i, n: (i, 0)),
            pl.BlockSpec((1, 1, d), lambda i, n: (n, 0, 0)),
            pl.BlockSpec((d, d), lambda i, n: (0, n)),
        ],
        out_specs=pl.BlockSpec((n_heads, tm, _HEAD_DIM), lambda i, n: (n, i, 0)),
        out_shape=jax.ShapeDtypeStruct((n_parts * n_heads, m, _HEAD_DIM), _BF16),
        compiler_params=pltpu.CompilerParams(
            dimension_semantics=("parallel", "arbitrary"),
            vmem_limit_bytes=_VMEM_LIMIT_BYTES),
    )(x2d, gains, w_cat)


def _log2_sigmoid(z2):
    return jnp.minimum(z2, 0.0) - jnp.log2(1.0 + jnp.exp2(-jnp.abs(z2)))


def _split_hi_lo(x):
    hi = x.astype(_BF16)
    lo = (x - hi.astype(_F32)).astype(_BF16)
    return jnp.concatenate([hi, lo], axis=1)


def _attn_kernel(q_ref, k_ref, v_ref, o_ref, lb_ref, lhs_ref, cs_ref, reach_ref, *, seq):
    t = _ATTN_BLOCK
    n_blocks = seq // t
    row = lax.broadcasted_iota(jnp.int32, (t, t), 0)
    col = lax.broadcasted_iota(jnp.int32, (t, t), 1)
    strictly_causal = col < row
    later = jnp.where(row > col, 1.0, 0.0).astype(_BF16)
    ones = jnp.ones((t, t), _BF16)
    sums = jnp.concatenate([jnp.concatenate([later, ones], axis=1)] * 2, axis=0)
    nt_dims = (((1,), (1,)), ((), ()))

    n_win = _ATTN_BACK_BLOCKS + 1

    def logs_phase(i, q0, n_back):
        w0 = q0 - n_back * t
        z_all = lax.dot_general(q_ref[0, pl.ds(q0, t), :], k_ref[0, pl.ds(w0, (n_back + 1) * t), :],
                                nt_dims, preferred_element_type=_F32)
        for jb in range(n_back + 1):
            z2 = z_all[:, (n_back - jb) * t:(n_back - jb + 1) * t]
            lb = _log2_sigmoid(z2)
            l1 = lb - z2
            if jb == 0:
                l1 = jnp.where(strictly_causal, l1, 0.0)
            lb_ref[i, :, jb * t:(jb + 1) * t] = lb
            lhs_ref[pl.ds((i * n_win + jb) * t, t), :] = _split_hi_lo(l1)
        for jb in range(n_back + 1, n_win):
            lhs_ref[pl.ds((i * n_win + jb) * t, t), :] = jnp.zeros((t, 2 * t), _BF16)

    def weights_phase(i, q0, n_back):
        w0 = q0 - n_back * t
        carry = None
        weights = []
        for jb in range(n_back + 1):
            rows = pl.ds((i * n_win + jb) * t, t)
            between = cs_ref[rows, :t] if carry is None else cs_ref[rows, :t] + carry
            a = jnp.exp2(lb_ref[i, :, jb * t:(jb + 1) * t] + between)
            if jb == 0:
                a = jnp.where(strictly_causal, a, 0.0)
            weights.append(a.astype(_BF16))
            carry = cs_ref[rows, t:] if carry is None else carry + cs_ref[rows, t:]
        a_all = jnp.concatenate(weights[::-1], axis=1)
        out = jnp.dot(a_all, v_ref[0, pl.ds(w0, (n_back + 1) * t), :], preferred_element_type=_F32)
        o_ref[0, pl.ds(q0, t), :] = out.astype(o_ref.dtype)
        reach_ref[i] = jnp.max(carry)

    def run_phase(phase, unroll):
        for i in range(_ATTN_BACK_BLOCKS):
            phase(i, i * t, i)

        def step(i, _):
            phase(i, pl.multiple_of(i * t, t), _ATTN_BACK_BLOCKS)
            return 0

        lax.fori_loop(_ATTN_BACK_BLOCKS, n_blocks, step, 0, unroll=unroll)

    run_phase(logs_phase, unroll=5)
    cs_ref[...] = jnp.dot(lhs_ref[...], sums, preferred_element_type=_F32)
    run_phase(weights_phase, unroll=5)

    def continue_block(i, _):
        @pl.when(reach_ref[i] > _LOG2_WEIGHT_FLOOR)
        def _():
            q0 = pl.multiple_of(i * t, t)
            q = q_ref[0, pl.ds(q0, t), :]
            carry = cs_ref[pl.ds(i * n_win * t, t), t:]
            for jb in range(1, n_win):
                carry = carry + cs_ref[pl.ds((i * n_win + jb) * t, t), t:]

            def cond(state):
                j, reach, _, _ = state
                return jnp.logical_and(j >= 0, reach > _LOG2_WEIGHT_FLOOR)

            def body(state):
                j, _, carry, acc = state
                k0 = pl.multiple_of(j * t, t)
                z2 = lax.dot_general(q, k_ref[0, pl.ds(k0, t), :], nt_dims, preferred_element_type=_F32)
                lb = _log2_sigmoid(z2)
                cs = jnp.dot(_split_hi_lo(lb - z2), sums, preferred_element_type=_F32)
                a = jnp.exp2(lb + cs[:, :t] + carry)
                acc = acc + jnp.dot(a.astype(_BF16), v_ref[0, pl.ds(k0, t), :],
                                    preferred_element_type=_F32)
                carry = carry + cs[:, t:]
                return j - 1, jnp.max(carry), carry, acc

            acc = o_ref[0, pl.ds(q0, t), :].astype(_F32)
            _, _, _, acc = lax.while_loop(cond, body, (i - n_win, reach_ref[i], carry, acc))
            o_ref[0, pl.ds(q0, t), :] = acc.astype(o_ref.dtype)

        return 0

    lax.fori_loop(n_win, n_blocks, continue_block, 0)


def _stick_breaking_attention(kvq, b, s, n_heads):
    t, n_win = _ATTN_BLOCK, _ATTN_BACK_BLOCKS + 1
    n_blocks = s // t
    assert s % t == 0 and n_blocks >= n_win
    return pl.pallas_call(
        functools.partial(_attn_kernel, seq=s),
        grid=(b, n_heads),
        in_specs=[
            pl.BlockSpec((1, s, _HEAD_DIM), lambda bi, h: (2 * n_heads + h, bi, 0)),
            pl.BlockSpec((1, s, _HEAD_DIM), lambda bi, h: (h, bi, 0)),
            pl.BlockSpec((1, s, _HEAD_DIM), lambda bi, h: (n_heads + h, bi, 0)),
        ],
        out_specs=pl.BlockSpec((1, s, _HEAD_DIM), lambda bi, h: (h, bi, 0)),
        out_shape=jax.ShapeDtypeStruct((n_heads, b * s, _HEAD_DIM), _BF16),
        scratch_shapes=[
            pltpu.VMEM((n_blocks, t, n_win * t), _F32),
            pltpu.VMEM((n_blocks * n_win * t, 2 * t), _BF16),
            pltpu.VMEM((n_blocks * n_win * t, 2 * t), _F32),
            pltpu.SMEM((n_blocks,), _F32),
        ],
        compiler_params=pltpu.CompilerParams(
            dimension_semantics=("parallel", "arbitrary"),
            vmem_limit_bytes=_VMEM_LIMIT_BYTES),
    )(kvq, kvq, kvq)


def _oproj_kernel(x_ref, o_ref_in, w_ref, out_ref):
    o = jnp.concatenate([o_ref_in[h] for h in range(o_ref_in.shape[0])], axis=1)
    out_ref[...] = x_ref[...] + jnp.dot(o, w_ref[...], preferred_element_type=_F32)


def _out_proj(x2d, o_heads, w_o, *, tm=512):
    m, d = x2d.shape
    n_heads = o_heads.shape[0]
    return pl.pallas_call(
        _oproj_kernel,
        grid=(m // tm,),
        in_specs=[
            pl.BlockSpec((tm, d), lambda i: (i, 0)),
            pl.BlockSpec((n_heads, tm, _HEAD_DIM), lambda i: (0, i, 0)),
            pl.BlockSpec((d, d), lambda i: (0, 0)),
        ],
        out_specs=pl.BlockSpec((tm, d), lambda i: (i, 0)),
        out_shape=jax.ShapeDtypeStruct((m, d), _F32),
        compiler_params=pltpu.CompilerParams(
            dimension_semantics=("parallel",),
            vmem_limit_bytes=_VMEM_LIMIT_BYTES),
    )(x2d, o_heads, w_o)


def kernel(x, pool_norm, pool_w, pool_scale, kv_norm, w_kv, attn_norm, w_q, w_o, mlp_norm,
           w_up, w_down, final_norm):
    b, s, d = x.shape
    n_heads = d // _HEAD_DIM
    m = b * s
    assert pool_norm.shape[0] == 1 and attn_norm.shape[0] == 1 and mlp_norm.shape[0] == 2

    x1 = _pool_mixer(x, pool_norm[0], pool_w[0].astype(_BF16), pool_scale[0])
    x2 = _mlp(x1.reshape(m, d), mlp_norm[0], w_up, w_down, 0)

    w_cat = jnp.concatenate([w_kv, w_q[0]], axis=1).astype(_BF16)
    q_scale = math.log2(math.e) / math.sqrt(_HEAD_DIM)
    gains = jnp.stack([kv_norm, kv_norm, attn_norm[0]]).reshape(3, 1, d)
    kvq = _kvq_proj(x2, gains, w_cat, q_scale, n_heads)
    o = _stick_breaking_attention(kvq, b, s, n_heads)
    x3 = _out_proj(x2, o, w_o[0].astype(_BF16))

    out = _mlp(x3, mlp_norm[1], w_up, w_down, 1, final_norm)
    return out.reshape(b, s, d)
```

```python
import functools
import math

import jax
import jax.numpy as jnp
from jax import lax
from jax.experimental import pallas as pl
from jax.experimental.pallas import tpu as pltpu

_EPS = 1e-6
_POOL_WINDOWS = (2, 4, 8, 16)
_HEAD_DIM = 128
_POOL_HALO = 16
_ATTN_BLOCK = 128
_ATTN_BACK_BLOCKS = 2
_ATTN_GROUP = 8
_LOG2_WEIGHT_FLOOR = -150.0
_VMEM_LIMIT_BYTES = 56 * 1024 * 1024

_F32 = jnp.float32
_BF16 = jnp.bfloat16


def _inv_rms(x):
    return lax.rsqrt(jnp.mean(x * x, axis=-1, keepdims=True) + _EPS)


def _pool_kernel(x_ref, halo_ref, g_ref, w_ref, sc_ref, o_ref, *, ts, group_dim):
    i = pl.program_id(1)
    g = g_ref[...]
    x = x_ref[0]
    xn = x * _inv_rms(x) * g
    h = halo_ref[0]
    hn = h * _inv_rms(h) * g
    hn = jnp.where(i > 0, hn, 0.0)
    ext = jnp.concatenate([hn, xn], axis=0)
    pos = i * ts + lax.broadcasted_iota(jnp.int32, (ts, 1), 0)
    for gi, w in enumerate(_POOL_WINDOWS):
        c0 = gi * group_dim
        s = ext[:, c0:c0 + group_dim]
        k = 1
        while k < w:
            s = s + pltpu.roll(s, k, axis=0)
            k *= 2
        win = s[_POOL_HALO:]
        inv_cnt = 1.0 / jnp.minimum(pos + 1, w).astype(_F32)
        diff = win * inv_cnt - xn[:, c0:c0 + group_dim]
        y = jnp.dot(diff.astype(_BF16), w_ref[gi], preferred_element_type=_F32)
        o_ref[0, :, c0:c0 + group_dim] = x[:, c0:c0 + group_dim] + y * sc_ref[:, c0:c0 + group_dim]


def _pool_mixer(x, norm_g, w_bf16, scale, *, ts=512):
    b, s, d = x.shape
    n_groups = len(_POOL_WINDOWS)
    group_dim = d // n_groups
    halo_blocks_per_tile = ts // _POOL_HALO
    kern = functools.partial(_pool_kernel, ts=ts, group_dim=group_dim)
    return pl.pallas_call(
        kern,
        grid=(b, s // ts),
        in_specs=[
            pl.BlockSpec((1, ts, d), lambda bi, i: (bi, i, 0)),
            pl.BlockSpec((1, _POOL_HALO, d),
                         lambda bi, i: (bi, jnp.maximum(i * halo_blocks_per_tile - 1, 0), 0)),
            pl.BlockSpec((1, d), lambda bi, i: (0, 0)),
            pl.BlockSpec((n_groups, group_dim, group_dim), lambda bi, i: (0, 0, 0)),
            pl.BlockSpec((1, d), lambda bi, i: (0, 0)),
        ],
        out_specs=pl.BlockSpec((1, ts, d), lambda bi, i: (bi, i, 0)),
        out_shape=jax.ShapeDtypeStruct((b, s, d), _F32),
        compiler_params=pltpu.CompilerParams(
            dimension_semantics=("parallel", "arbitrary"),
            vmem_limit_bytes=_VMEM_LIMIT_BYTES),
    )(x, x, norm_g.reshape(1, d), w_bf16, scale.reshape(1, d))


def _mlp_kernel(x_ref, g_ref, wu_ref, wd_ref, *rest, final_norm):
    if final_norm:
        gf_ref, o_ref, xn_ref = rest
    else:
        o_ref, xn_ref = rest
    f = pl.program_id(1)

    @pl.when(f == 0)
    def _():
        x = x_ref[...]
        xn_ref[...] = (x * _inv_rms(x) * g_ref[...]).astype(_BF16)
        o_ref[...] = x

    h = jnp.dot(xn_ref[...], wu_ref[0].astype(_BF16), preferred_element_type=_F32)
    h = jnp.maximum(h, 0.0)
    h = (h * h).astype(_BF16)
    o_ref[...] += jnp.dot(h, wd_ref[0].astype(_BF16), preferred_element_type=_F32)

    if final_norm:
        @pl.when(f == pl.num_programs(1) - 1)
        def _():
            r = o_ref[...]
            o_ref[...] = r * _inv_rms(r) * gf_ref[...]


def _mlp(x2d, norm_g, w_up, w_down, layer, final_g=None, *, tm=1024, tf=512):
    m, d = x2d.shape
    d_ff = w_up.shape[2]
    final_norm = final_g is not None
    in_specs = [
        pl.BlockSpec((tm, d), lambda i, f: (i, 0)),
        pl.BlockSpec((1, d), lambda i, f: (0, 0)),
        pl.BlockSpec((1, d, tf), lambda i, f: (layer, 0, f)),
        pl.BlockSpec((1, tf, d), lambda i, f: (layer, f, 0)),
    ]
    args = [x2d, norm_g.reshape(1, d), w_up, w_down]
    if final_norm:
        in_specs.append(pl.BlockSpec((1, d), lambda i, f: (0, 0)))
        args.append(final_g.reshape(1, d))
    return pl.pallas_call(
        functools.partial(_mlp_kernel, final_norm=final_norm),
        grid=(m // tm, d_ff // tf),
        in_specs=in_specs,
        out_specs=pl.BlockSpec((tm, d), lambda i, f: (i, 0)),
        out_shape=jax.ShapeDtypeStruct((m, d), _F32),
        scratch_shapes=[pltpu.VMEM((tm, d), _BF16)],
        compiler_params=pltpu.CompilerParams(
            dimension_semantics=("parallel", "arbitrary"),
            vmem_limit_bytes=_VMEM_LIMIT_BYTES),
    )(*args)


def _proj_kernel(x_ref, g_ref, w_ref, o_ref, *, q_part, q_scale):
    x = x_ref[...]
    xn = (x * _inv_rms(x) * g_ref[0]).astype(_BF16)
    y = jnp.dot(xn, w_ref[...], preferred_element_type=_F32)
    y = y * jnp.where(pl.program_id(1) == q_part, q_scale, 1.0)
    for h in range(o_ref.shape[0]):
        o_ref[h] = y[:, h * _HEAD_DIM:(h + 1) * _HEAD_DIM].astype(_BF16)


def _kvq_proj(x2d, gains, w_cat, q_scale, n_heads, *, tm=1024):
    m, d = x2d.shape
    n_parts = gains.shape[0]
    assert w_cat.shape == (d, n_parts * d) and d == n_heads * _HEAD_DIM
    return pl.pallas_call(
        functools.partial(_proj_kernel, q_part=n_parts - 1, q_scale=q_scale),
        grid=(m // tm, n_parts),
        in_specs=[
            pl.BlockSpec((tm, d), lambda i, n: (i, 0)),
            pl.BlockSpec((1, 1, d), lambda i, n: (n, 0, 0)),
            pl.BlockSpec((d, d), lambda i, n: (0, n)),
        ],
        out_specs=pl.BlockSpec((n_heads, tm, _HEAD_DIM), lambda i, n: (n, i, 0)),
        out_shape=jax.ShapeDtypeStruct((n_parts * n_heads, m, _HEAD_DIM), _BF16),
        compiler_params=pltpu.CompilerParams(
            dimension_semantics=("parallel", "arbitrary"),
            vmem_limit_bytes=_VMEM_LIMIT_BYTES),
    )(x2d, gains, w_cat)


def _log2_sigmoid(z2):
    return jnp.minimum(z2, 0.0) - jnp.log2(1.0 + jnp.exp2(-jnp.abs(z2)))


def _split_hi_lo(x):
    hi = x.astype(_BF16)
    lo = (x - hi.astype(_F32)).astype(_BF16)
    return jnp.concatenate([hi, lo], axis=1)


def _attn_kernel(q_ref, k_ref, v_ref, o_ref, lb_ref, lhs_ref, cs_ref, reach_ref, *, seq):
    t = _ATTN_BLOCK
    n_blocks = seq // t
    row = lax.broadcasted_iota(jnp.int32, (t, t), 0)
    col = lax.broadcasted_iota(jnp.int32, (t, t), 1)
    strictly_causal = col < row
    later = jnp.where(row > col, 1.0, 0.0).astype(_BF16)
    ones = jnp.ones((t, t), _BF16)
    sums = jnp.concatenate([jnp.concatenate([later, ones], axis=1)] * 2, axis=0)
    nt_dims = (((1,), (1,)), ((), ()))

    n_win = _ATTN_BACK_BLOCKS + 1

    def logs_phase(i, q0, n_back):
        w0 = q0 - n_back * t
        z_all = lax.dot_general(q_ref[0, pl.ds(q0, t), :], k_ref[0, pl.ds(w0, (n_back + 1) * t), :],
                                nt_dims, preferred_element_type=_F32)
        for jb in range(n_back + 1):
            z2 = z_all[:, (n_back - jb) * t:(n_back - jb + 1) * t]
            lb = _log2_sigmoid(z2)
            l1 = lb - z2
            if jb == 0:
                l1 = jnp.where(strictly_causal, l1, 0.0)
            lb_ref[i, :, jb * t:(jb + 1) * t] = lb
            lhs_ref[pl.ds((i * n_win + jb) * t, t), :] = _split_hi_lo(l1)
        for jb in range(n_back + 1, n_win):
            lhs_ref[pl.ds((i * n_win + jb) * t, t), :] = jnp.zeros((t, 2 * t), _BF16)

    def weights_phase(i, q0, n_back):
        w0 = q0 - n_back * t
        carry = None
        weights = []
        for jb in range(n_back + 1):
            rows = pl.ds((i * n_win + jb) * t, t)
            between = cs_ref[rows, :t] if carry is None else cs_ref[rows, :t] + carry
            a = jnp.exp2(lb_ref[i, :, jb * t:(jb + 1) * t] + between)
            if jb == 0:
                a = jnp.where(strictly_causal, a, 0.0)
            weights.append(a.astype(_BF16))
            carry = cs_ref[rows, t:] if carry is None else carry + cs_ref[rows, t:]
        a_all = jnp.concatenate(weights[::-1], axis=1)
        out = jnp.dot(a_all, v_ref[0, pl.ds(w0, (n_back + 1) * t), :], preferred_element_type=_F32)
        o_ref[0, pl.ds(q0, t), :] = out.astype(o_ref.dtype)
        reach_ref[i] = jnp.max(carry)

    def sums_phase(blocks):
        rows = slice(blocks[0] * n_win * t, (blocks[-1] + 1) * n_win * t)
        cs_ref[rows, :] = jnp.dot(lhs_ref[rows, :], sums, preferred_element_type=_F32)

    groups = [list(range(g, min(g + _ATTN_GROUP, n_blocks))) for g in range(0, n_blocks, _ATTN_GROUP)]
    for step in range(len(groups) + 2):
        if step >= 2:
            for i in groups[step - 2]:
                weights_phase(i, i * t, min(i, _ATTN_BACK_BLOCKS))
        if 1 <= step <= len(groups):
            sums_phase(groups[step - 1])
        if step < len(groups):
            for i in groups[step]:
                logs_phase(i, i * t, min(i, _ATTN_BACK_BLOCKS))

    def continue_block(i, _):
        @pl.when(reach_ref[i] > _LOG2_WEIGHT_FLOOR)
        def _():
            q0 = pl.multiple_of(i * t, t)
            q = q_ref[0, pl.ds(q0, t), :]
            carry = cs_ref[pl.ds(i * n_win * t, t), t:]
            for jb in range(1, n_win):
                carry = carry + cs_ref[pl.ds((i * n_win + jb) * t, t), t:]

            def cond(state):
                j, reach, _, _ = state
                return jnp.logical_and(j >= 0, reach > _LOG2_WEIGHT_FLOOR)

            def body(state):
                j, _, carry, acc = state
                k0 = pl.multiple_of(j * t, t)
                z2 = lax.dot_general(q, k_ref[0, pl.ds(k0, t), :], nt_dims, preferred_element_type=_F32)
                lb = _log2_sigmoid(z2)
                cs = jnp.dot(_split_hi_lo(lb - z2), sums, preferred_element_type=_F32)
                a = jnp.exp2(lb + cs[:, :t] + carry)
                acc = acc + jnp.dot(a.astype(_BF16), v_ref[0, pl.ds(k0, t), :],
                                    preferred_element_type=_F32)
                carry = carry + cs[:, t:]
                return j - 1, jnp.max(carry), carry, acc

            acc = o_ref[0, pl.ds(q0, t), :].astype(_F32)
            _, _, _, acc = lax.while_loop(cond, body, (i - n_win, reach_ref[i], carry, acc))
            o_ref[0, pl.ds(q0, t), :] = acc.astype(o_ref.dtype)

        return 0

    lax.fori_loop(n_win, n_blocks, continue_block, 0)


def _stick_breaking_attention(kvq, b, s, n_heads):
    t, n_win = _ATTN_BLOCK, _ATTN_BACK_BLOCKS + 1
    n_blocks = s // t
    assert s % t == 0 and n_blocks >= n_win
    return pl.pallas_call(
        functools.partial(_attn_kernel, seq=s),
        grid=(b, n_heads),
        in_specs=[
            pl.BlockSpec((1, s, _HEAD_DIM), lambda bi, h: (2 * n_heads + h, bi, 0)),
            pl.BlockSpec((1, s, _HEAD_DIM), lambda bi, h: (h, bi, 0)),
            pl.BlockSpec((1, s, _HEAD_DIM), lambda bi, h: (n_heads + h, bi, 0)),
        ],
        out_specs=pl.BlockSpec((1, s, _HEAD_DIM), lambda bi, h: (h, bi, 0)),
        out_shape=jax.ShapeDtypeStruct((n_heads, b * s, _HEAD_DIM), _BF16),
        scratch_shapes=[
            pltpu.VMEM((n_blocks, t, n_win * t), _F32),
            pltpu.VMEM((n_blocks * n_win * t, 2 * t), _BF16),
            pltpu.VMEM((n_blocks * n_win * t, 2 * t), _F32),
            pltpu.SMEM((n_blocks,), _F32),
        ],
        compiler_params=pltpu.CompilerParams(
            dimension_semantics=("parallel", "arbitrary"),
            vmem_limit_bytes=_VMEM_LIMIT_BYTES),
    )(kvq, kvq, kvq)


def _oproj_kernel(x_ref, o_ref_in, w_ref, out_ref):
    o = jnp.concatenate([o_ref_in[h] for h in range(o_ref_in.shape[0])], axis=1)
    out_ref[...] = x_ref[...] + jnp.dot(o, w_ref[...], preferred_element_type=_F32)


def _out_proj(x2d, o_heads, w_o, *, tm=512):
    m, d = x2d.shape
    n_heads = o_heads.shape[0]
    return pl.pallas_call(
        _oproj_kernel,
        grid=(m // tm,),
        in_specs=[
            pl.BlockSpec((tm, d), lambda i: (i, 0)),
            pl.BlockSpec((n_heads, tm, _HEAD_DIM), lambda i: (0, i, 0)),
            pl.BlockSpec((d, d), lambda i: (0, 0)),
        ],
        out_specs=pl.BlockSpec((tm, d), lambda i: (i, 0)),
        out_shape=jax.ShapeDtypeStruct((m, d), _F32),
        compiler_params=pltpu.CompilerParams(
            dimension_semantics=("parallel",),
            vmem_limit_bytes=_VMEM_LIMIT_BYTES),
    )(x2d, o_heads, w_o)


def kernel(x, pool_norm, pool_w, pool_scale, kv_norm, w_kv, attn_norm, w_q, w_o, mlp_norm,
           w_up, w_down, final_norm):
    b, s, d = x.shape
    n_heads = d // _HEAD_DIM
    m = b * s
    assert pool_norm.shape[0] == 1 and attn_norm.shape[0] == 1 and mlp_norm.shape[0] == 2

    x1 = _pool_mixer(x, pool_norm[0], pool_w[0].astype(_BF16), pool_scale[0])
    x2 = _mlp(x1.reshape(m, d), mlp_norm[0], w_up, w_down, 0)

    w_cat = jnp.concatenate([w_kv, w_q[0]], axis=1).astype(_BF16)
    q_scale = math.log2(math.e) / math.sqrt(_HEAD_DIM)
    gains = jnp.stack([kv_norm, kv_norm, attn_norm[0]]).reshape(3, 1, d)
    kvq = _kvq_proj(x2, gains, w_cat, q_scale, n_heads)
    o = _stick_breaking_attention(kvq, b, s, n_heads)
    x3 = _out_proj(x2, o, w_o[0].astype(_BF16))

    out = _mlp(x3, mlp_norm[1], w_up, w_down, 1, final_norm)
    return out.reshape(b, s, d)
```

```python
import functools
import math

import jax
import jax.numpy as jnp
from jax import lax
from jax.experimental import pallas as pl
from jax.experimental.pallas import tpu as pltpu

_EPS = 1e-6
_POOL_WINDOWS = (2, 4, 8, 16)
_HEAD_DIM = 128
_POOL_HALO = 16
_ATTN_BLOCK = 128
_ATTN_BACK_BLOCKS = 2
_ATTN_GROUP = 8
_LOG2_WEIGHT_FLOOR = -150.0
_VMEM_LIMIT_BYTES = 56 * 1024 * 1024

_F32 = jnp.float32
_BF16 = jnp.bfloat16


def _inv_rms(x):
    return lax.rsqrt(jnp.mean(x * x, axis=-1, keepdims=True) + _EPS)


def _pool_kernel(x_ref, halo_ref, g_ref, w_ref, sc_ref, o_ref, *, ts, group_dim):
    i = pl.program_id(1)
    g = g_ref[...]
    x = x_ref[0]
    xn = x * _inv_rms(x) * g
    h = halo_ref[0]
    hn = h * _inv_rms(h) * g
    hn = jnp.where(i > 0, hn, 0.0)
    ext = jnp.concatenate([hn, xn], axis=0)
    pos = i * ts + lax.broadcasted_iota(jnp.int32, (ts, 1), 0)
    for gi, w in enumerate(_POOL_WINDOWS):
        c0 = gi * group_dim
        s = ext[:, c0:c0 + group_dim]
        k = 1
        while k < w:
            s = s + pltpu.roll(s, k, axis=0)
            k *= 2
        win = s[_POOL_HALO:]
        inv_cnt = 1.0 / jnp.minimum(pos + 1, w).astype(_F32)
        diff = win * inv_cnt - xn[:, c0:c0 + group_dim]
        y = jnp.dot(diff.astype(_BF16), w_ref[gi], preferred_element_type=_F32)
        o_ref[0, :, c0:c0 + group_dim] = x[:, c0:c0 + group_dim] + y * sc_ref[:, c0:c0 + group_dim]


def _pool_mixer(x, norm_g, w_bf16, scale, *, ts=512):
    b, s, d = x.shape
    n_groups = len(_POOL_WINDOWS)
    group_dim = d // n_groups
    halo_blocks_per_tile = ts // _POOL_HALO
    kern = functools.partial(_pool_kernel, ts=ts, group_dim=group_dim)
    return pl.pallas_call(
        kern,
        grid=(b, s // ts),
        in_specs=[
            pl.BlockSpec((1, ts, d), lambda bi, i: (bi, i, 0)),
            pl.BlockSpec((1, _POOL_HALO, d),
                         lambda bi, i: (bi, jnp.maximum(i * halo_blocks_per_tile - 1, 0), 0)),
            pl.BlockSpec((1, d), lambda bi, i: (0, 0)),
            pl.BlockSpec((n_groups, group_dim, group_dim), lambda bi, i: (0, 0, 0)),
            pl.BlockSpec((1, d), lambda bi, i: (0, 0)),
        ],
        out_specs=pl.BlockSpec((1, ts, d), lambda bi, i: (bi, i, 0)),
        out_shape=jax.ShapeDtypeStruct((b, s, d), _F32),
        compiler_params=pltpu.CompilerParams(
            dimension_semantics=("parallel", "arbitrary"),
            vmem_limit_bytes=_VMEM_LIMIT_BYTES),
    )(x, x, norm_g.reshape(1, d), w_bf16, scale.reshape(1, d))


def _mlp_kernel(x_ref, g_ref, wu_ref, wd_ref, *rest, final_norm):
    if final_norm:
        gf_ref, o_ref, xn_ref = rest
    else:
        o_ref, xn_ref = rest
    f = pl.program_id(1)

    def chunk(xn, base):
        h = jnp.dot(xn, wu_ref[0].astype(_BF16), preferred_element_type=_F32)
        h = jnp.maximum(h, 0.0)
        h = (h * h).astype(_BF16)
        o_ref[...] = base + jnp.dot(h, wd_ref[0].astype(_BF16), preferred_element_type=_F32)

    @pl.when(f == 0)
    def _():
        x = x_ref[...]
        xn = (x * _inv_rms(x) * g_ref[...]).astype(_BF16)
        xn_ref[...] = xn
        chunk(xn, x)

    @pl.when(f > 0)
    def _():
        chunk(xn_ref[...], o_ref[...])

    if final_norm:
        @pl.when(f == pl.num_programs(1) - 1)
        def _():
            r = o_ref[...]
            o_ref[...] = r * _inv_rms(r) * gf_ref[...]


def _mlp(x2d, norm_g, w_up, w_down, layer, final_g=None, *, tm=1024, tf=512):
    m, d = x2d.shape
    d_ff = w_up.shape[2]
    final_norm = final_g is not None
    in_specs = [
        pl.BlockSpec((tm, d), lambda i, f: (i, 0)),
        pl.BlockSpec((1, d), lambda i, f: (0, 0)),
        pl.BlockSpec((1, d, tf), lambda i, f: (layer, 0, f)),
        pl.BlockSpec((1, tf, d), lambda i, f: (layer, f, 0)),
    ]
    args = [x2d, norm_g.reshape(1, d), w_up, w_down]
    if final_norm:
        in_specs.append(pl.BlockSpec((1, d), lambda i, f: (0, 0)))
        args.append(final_g.reshape(1, d))
    return pl.pallas_call(
        functools.partial(_mlp_kernel, final_norm=final_norm),
        grid=(m // tm, d_ff // tf),
        in_specs=in_specs,
        out_specs=pl.BlockSpec((tm, d), lambda i, f: (i, 0)),
        out_shape=jax.ShapeDtypeStruct((m, d), _F32),
        scratch_shapes=[pltpu.VMEM((tm, d), _BF16)],
        compiler_params=pltpu.CompilerParams(
            dimension_semantics=("parallel", "arbitrary"),
            vmem_limit_bytes=_VMEM_LIMIT_BYTES),
    )(*args)


def _proj_kernel(x_ref, g_ref, w_ref, o_ref, *, q_part, q_scale):
    x = x_ref[...]
    xn = (x * _inv_rms(x) * g_ref[0]).astype(_BF16)
    y = jnp.dot(xn, w_ref[...], preferred_element_type=_F32)
    y = y * jnp.where(pl.program_id(1) == q_part, q_scale, 1.0)
    for h in range(o_ref.shape[0]):
        o_ref[h] = y[:, h * _HEAD_DIM:(h + 1) * _HEAD_DIM].astype(_BF16)


def _kvq_proj(x2d, gains, w_cat, q_scale, n_heads, *, tm=1024):
    m, d = x2d.shape
    n_parts = gains.shape[0]
    assert w_cat.shape == (d, n_parts * d) and d == n_heads * _HEAD_DIM
    return pl.pallas_call(
        functools.partial(_proj_kernel, q_part=n_parts - 1, q_scale=q_scale),
        grid=(m // tm, n_parts),
        in_specs=[
            pl.BlockSpec((tm, d), lambda i, n: (i, 0)),
            pl.BlockSpec((1, 1, d), lambda i, n: (n, 0, 0)),
            pl.BlockSpec((d, d), lambda i, n: (0, n)),
        ],
        out_specs=pl.BlockSpec((n_heads, tm, _HEAD_DIM), lambda i, n: (n, i, 0)),
        out_shape=jax.ShapeDtypeStruct((n_parts * n_heads, m, _HEAD_DIM), _BF16),
        compiler_params=pltpu.CompilerParams(
            dimension_semantics=("parallel", "arbitrary"),
            vmem_limit_bytes=_VMEM_LIMIT_BYTES),
    )(x2d, gains, w_cat)


def _log2_sigmoid(z2):
    return jnp.minimum(z2, 0.0) - jnp.log2(1.0 + jnp.exp2(-jnp.abs(z2)))


def _split_hi_lo(x):
    hi = x.astype(_BF16)
    lo = (x - hi.astype(_F32)).astype(_BF16)
    return jnp.concatenate([hi, lo], axis=1)


def _attn_kernel(q_ref, k_ref, v_ref, o_ref, lb_ref, lhs_ref, cs_ref, reach_ref, *, seq):
    t = _ATTN_BLOCK
    n_blocks = seq // t
    row = lax.broadcasted_iota(jnp.int32, (t, t), 0)
    col = lax.broadcasted_iota(jnp.int32, (t, t), 1)
    strictly_causal = col < row
    later = jnp.where(row > col, 1.0, 0.0).astype(_BF16)
    ones = jnp.ones((t, t), _BF16)
    sums = jnp.concatenate([jnp.concatenate([later, ones], axis=1)] * 2, axis=0)
    nt_dims = (((1,), (1,)), ((), ()))

    n_win = _ATTN_BACK_BLOCKS + 1

    def logs_phase(i, q0, n_back):
        w0 = q0 - n_back * t
        z_all = lax.dot_general(q_ref[0, pl.ds(q0, t), :], k_ref[0, pl.ds(w0, (n_back + 1) * t), :],
                                nt_dims, preferred_element_type=_F32)
        for jb in range(n_back + 1):
            z2 = z_all[:, (n_back - jb) * t:(n_back - jb + 1) * t]
            lb = _log2_sigmoid(z2)
            l1 = lb - z2
            if jb == 0:
                l1 = jnp.where(strictly_causal, l1, 0.0)
            lb_ref[i, :, jb * t:(jb + 1) * t] = lb
            lhs_ref[pl.ds((i * n_win + jb) * t, t), :] = _split_hi_lo(l1)
        for jb in range(n_back + 1, n_win):
            lhs_ref[pl.ds((i * n_win + jb) * t, t), :] = jnp.zeros((t, 2 * t), _BF16)

    def weights_phase(i, q0, n_back):
        w0 = q0 - n_back * t
        carry = None
        weights = []
        for jb in range(n_back + 1):
            rows = pl.ds((i * n_win + jb) * t, t)
            between = cs_ref[rows, :t] if carry is None else cs_ref[rows, :t] + carry
            a = jnp.exp2(lb_ref[i, :, jb * t:(jb + 1) * t] + between)
            if jb == 0:
                a = jnp.where(strictly_causal, a, 0.0)
            weights.append(a.astype(_BF16))
            carry = cs_ref[rows, t:] if carry is None else carry + cs_ref[rows, t:]
        a_all = jnp.concatenate(weights[::-1], axis=1)
        out = jnp.dot(a_all, v_ref[0, pl.ds(w0, (n_back + 1) * t), :], preferred_element_type=_F32)
        o_ref[0, pl.ds(q0, t), :] = out.astype(o_ref.dtype)
        reach_ref[i] = jnp.max(carry)

    def sums_phase(blocks):
        rows = slice(blocks[0] * n_win * t, (blocks[-1] + 1) * n_win * t)
        cs_ref[rows, :] = jnp.dot(lhs_ref[rows, :], sums, preferred_element_type=_F32)

    groups = [list(range(g, min(g + _ATTN_GROUP, n_blocks))) for g in range(0, n_blocks, _ATTN_GROUP)]
    for step in range(len(groups) + 2):
        if step >= 2:
            for i in groups[step - 2]:
                weights_phase(i, i * t, min(i, _ATTN_BACK_BLOCKS))
        if 1 <= step <= len(groups):
            sums_phase(groups[step - 1])
        if step < len(groups):
            for i in groups[step]:
                logs_phase(i, i * t, min(i, _ATTN_BACK_BLOCKS))

    def continue_block(i, _):
        @pl.when(reach_ref[i] > _LOG2_WEIGHT_FLOOR)
        def _():
            q0 = pl.multiple_of(i * t, t)
            q = q_ref[0, pl.ds(q0, t), :]
            carry = cs_ref[pl.ds(i * n_win * t, t), t:]
            for jb in range(1, n_win):
                carry = carry + cs_ref[pl.ds((i * n_win + jb) * t, t), t:]

            def cond(state):
                j, reach, _, _ = state
                return jnp.logical_and(j >= 0, reach > _LOG2_WEIGHT_FLOOR)

            def body(state):
                j, _, carry, acc = state
                k0 = pl.multiple_of(j * t, t)
                z2 = lax.dot_general(q, k_ref[0, pl.ds(k0, t), :], nt_dims, preferred_element_type=_F32)
                lb = _log2_sigmoid(z2)
                cs = jnp.dot(_split_hi_lo(lb - z2), sums, preferred_element_type=_F32)
                a = jnp.exp2(lb + cs[:, :t] + carry)
                acc = acc + jnp.dot(a.astype(_BF16), v_ref[0, pl.ds(k0, t), :],
                                    preferred_element_type=_F32)
                carry = carry + cs[:, t:]
                return j - 1, jnp.max(carry), carry, acc

            acc = o_ref[0, pl.ds(q0, t), :].astype(_F32)
            _, _, _, acc = lax.while_loop(cond, body, (i - n_win, reach_ref[i], carry, acc))
            o_ref[0, pl.ds(q0, t), :] = acc.astype(o_ref.dtype)

        return 0

    lax.fori_loop(n_win, n_blocks, continue_block, 0)


def _stick_breaking_attention(kvq, b, s, n_heads):
    t, n_win = _ATTN_BLOCK, _ATTN_BACK_BLOCKS + 1
    n_blocks = s // t
    assert s % t == 0 and n_blocks >= n_win
    return pl.pallas_call(
        functools.partial(_attn_kernel, seq=s),
        grid=(b, n_heads),
        in_specs=[
            pl.BlockSpec((1, s, _HEAD_DIM), lambda bi, h: (2 * n_heads + h, bi, 0)),
            pl.BlockSpec((1, s, _HEAD_DIM), lambda bi, h: (h, bi, 0)),
            pl.BlockSpec((1, s, _HEAD_DIM), lambda bi, h: (n_heads + h, bi, 0)),
        ],
        out_specs=pl.BlockSpec((1, s, _HEAD_DIM), lambda bi, h: (h, bi, 0)),
        out_shape=jax.ShapeDtypeStruct((n_heads, b * s, _HEAD_DIM), _BF16),
        scratch_shapes=[
            pltpu.VMEM((n_blocks, t, n_win * t), _F32),
            pltpu.VMEM((n_blocks * n_win * t, 2 * t), _BF16),
            pltpu.VMEM((n_blocks * n_win * t, 2 * t), _F32),
            pltpu.SMEM((n_blocks,), _F32),
        ],
        compiler_params=pltpu.CompilerParams(
            dimension_semantics=("parallel", "arbitrary"),
            vmem_limit_bytes=_VMEM_LIMIT_BYTES),
    )(kvq, kvq, kvq)


def _oproj_kernel(x_ref, o_ref_in, w_ref, out_ref):
    o = jnp.concatenate([o_ref_in[h] for h in range(o_ref_in.shape[0])], axis=1)
    out_ref[...] = x_ref[...] + jnp.dot(o, w_ref[...], preferred_element_type=_F32)


def _out_proj(x2d, o_heads, w_o, *, tm=512):
    m, d = x2d.shape
    n_heads = o_heads.shape[0]
    return pl.pallas_call(
        _oproj_kernel,
        grid=(m // tm,),
        in_specs=[
            pl.BlockSpec((tm, d), lambda i: (i, 0)),
            pl.BlockSpec((n_heads, tm, _HEAD_DIM), lambda i: (0, i, 0)),
            pl.BlockSpec((d, d), lambda i: (0, 0)),
        ],
        out_specs=pl.BlockSpec((tm, d), lambda i: (i, 0)),
        out_shape=jax.ShapeDtypeStruct((m, d), _F32),
        compiler_params=pltpu.CompilerParams(
            dimension_semantics=("parallel",),
            vmem_limit_bytes=_VMEM_LIMIT_BYTES),
    )(x2d, o_heads, w_o)


def kernel(x, pool_norm, pool_w, pool_scale, kv_norm, w_kv, attn_norm, w_q, w_o, mlp_norm,
           w_up, w_down, final_norm):
    b, s, d = x.shape
    n_heads = d // _HEAD_DIM
    m = b * s
    assert pool_norm.shape[0] == 1 and attn_norm.shape[0] == 1 and mlp_norm.shape[0] == 2

    x1 = _pool_mixer(x, pool_norm[0], pool_w[0].astype(_BF16), pool_scale[0])
    x2 = _mlp(x1.reshape(m, d), mlp_norm[0], w_up, w_down, 0)

    w_cat = jnp.concatenate([w_kv.astype(_BF16), w_q[0].astype(_BF16)], axis=1)
    q_scale = math.log2(math.e) / math.sqrt(_HEAD_DIM)
    gains = jnp.stack([kv_norm, kv_norm, attn_norm[0]]).reshape(3, 1, d)
    kvq = _kvq_proj(x2, gains, w_cat, q_scale, n_heads)
    o = _stick_breaking_attention(kvq, b, s, n_heads)
    x3 = _out_proj(x2, o, w_o[0].astype(_BF16))

    out = _mlp(x3, mlp_norm[1], w_up, w_down, 1, final_norm)
    return out.reshape(b, s, d)
```

```python
import functools
import math

import jax
import jax.numpy as jnp
from jax import lax
from jax.experimental import pallas as pl
from jax.experimental.pallas import tpu as pltpu

_EPS = 1e-6
_POOL_WINDOWS = (2, 4, 8, 16)
_HEAD_DIM = 128
_POOL_HALO = 16
_ATTN_BLOCK = 128
_ATTN_BACK_BLOCKS = 2
_ATTN_GROUP = 8
_LOG2_WEIGHT_FLOOR = -150.0
_VMEM_LIMIT_BYTES = 56 * 1024 * 1024

_F32 = jnp.float32
_BF16 = jnp.bfloat16


def _inv_rms(x):
    return lax.rsqrt(jnp.mean(x * x, axis=-1, keepdims=True) + _EPS)


def _pool_kernel(x_ref, halo_ref, g_ref, w_ref, sc_ref, o_ref, *, ts, group_dim):
    i = pl.program_id(1)
    g = g_ref[...]
    x = x_ref[0]
    xn = x * _inv_rms(x) * g
    h = halo_ref[0]
    hn = h * _inv_rms(h) * g
    hn = jnp.where(i > 0, hn, 0.0)
    ext = jnp.concatenate([hn, xn], axis=0)
    pos = i * ts + lax.broadcasted_iota(jnp.int32, (ts, 1), 0)
    for gi, w in enumerate(_POOL_WINDOWS):
        c0 = gi * group_dim
        s = ext[:, c0:c0 + group_dim]
        k = 1
        while k < w:
            s = s + pltpu.roll(s, k, axis=0)
            k *= 2
        win = s[_POOL_HALO:]
        inv_cnt = 1.0 / jnp.minimum(pos + 1, w).astype(_F32)
        diff = win * inv_cnt - xn[:, c0:c0 + group_dim]
        y = jnp.dot(diff.astype(_BF16), w_ref[gi], preferred_element_type=_F32)
        o_ref[0, :, c0:c0 + group_dim] = x[:, c0:c0 + group_dim] + y * sc_ref[:, c0:c0 + group_dim]


def _pool_mixer(x, norm_g, w_bf16, scale, *, ts=512):
    b, s, d = x.shape
    n_groups = len(_POOL_WINDOWS)
    group_dim = d // n_groups
    halo_blocks_per_tile = ts // _POOL_HALO
    kern = functools.partial(_pool_kernel, ts=ts, group_dim=group_dim)
    return pl.pallas_call(
        kern,
        grid=(b, s // ts),
        in_specs=[
            pl.BlockSpec((1, ts, d), lambda bi, i: (bi, i, 0)),
            pl.BlockSpec((1, _POOL_HALO, d),
                         lambda bi, i: (bi, jnp.maximum(i * halo_blocks_per_tile - 1, 0), 0)),
            pl.BlockSpec((1, d), lambda bi, i: (0, 0)),
            pl.BlockSpec((n_groups, group_dim, group_dim), lambda bi, i: (0, 0, 0)),
            pl.BlockSpec((1, d), lambda bi, i: (0, 0)),
        ],
        out_specs=pl.BlockSpec((1, ts, d), lambda bi, i: (bi, i, 0)),
        out_shape=jax.ShapeDtypeStruct((b, s, d), _F32),
        compiler_params=pltpu.CompilerParams(
            dimension_semantics=("parallel", "arbitrary"),
            vmem_limit_bytes=_VMEM_LIMIT_BYTES),
    )(x, x, norm_g.reshape(1, d), w_bf16, scale.reshape(1, d))


def _mlp_kernel(x_ref, g_ref, wu_hbm, wd_hbm, *rest, layer, tf, n_chunks, final_norm):
    if final_norm:
        gf_ref, o_ref, xn_ref, wu_buf, wd_buf, sem = rest
    else:
        o_ref, xn_ref, wu_buf, wd_buf, sem = rest
    i = pl.program_id(0)

    def chunk_copies(c, slot):
        off = c * tf if isinstance(c, int) else pl.multiple_of(c * tf, tf)
        return (
            pltpu.make_async_copy(wu_hbm.at[layer, :, pl.ds(off, tf)], wu_buf.at[slot], sem.at[0, slot]),
            pltpu.make_async_copy(wd_hbm.at[layer, pl.ds(off, tf), :], wd_buf.at[slot], sem.at[1, slot]),
        )

    @pl.when(i == 0)
    def _():
        for cp in chunk_copies(0, 0):
            cp.start()

    x = x_ref[...]
    xn_ref[...] = (x * _inv_rms(x) * g_ref[...]).astype(_BF16)
    o_ref[...] = x

    def chunk(c, slot):
        for cp in chunk_copies(c, slot):
            cp.wait()
        for cp in chunk_copies(jnp.where(c + 1 < n_chunks, c + 1, 0), 1 - slot):
            cp.start()
        h = jnp.dot(xn_ref[...], wu_buf[slot].astype(_BF16), preferred_element_type=_F32)
        h = jnp.maximum(h, 0.0)
        h = (h * h).astype(_BF16)
        o_ref[...] += jnp.dot(h, wd_buf[slot].astype(_BF16), preferred_element_type=_F32)

    def chunk_pair(p, _):
        chunk(2 * p, 0)
        chunk(2 * p + 1, 1)
        return 0

    lax.fori_loop(0, n_chunks // 2, chunk_pair, 0)

    @pl.when(i == pl.num_programs(0) - 1)
    def _():
        for cp in chunk_copies(0, 0):
            cp.wait()

    if final_norm:
        r = o_ref[...]
        o_ref[...] = r * _inv_rms(r) * gf_ref[...]


def _mlp(x2d, norm_g, w_up, w_down, layer, final_g=None, *, tm=1024, tf=512):
    m, d = x2d.shape
    d_ff = w_up.shape[2]
    n_chunks = d_ff // tf
    assert n_chunks % 2 == 0
    final_norm = final_g is not None
    in_specs = [
        pl.BlockSpec((tm, d), lambda i: (i, 0)),
        pl.BlockSpec((1, d), lambda i: (0, 0)),
        pl.BlockSpec(memory_space=pl.ANY),
        pl.BlockSpec(memory_space=pl.ANY),
    ]
    args = [x2d, norm_g.reshape(1, d), w_up, w_down]
    if final_norm:
        in_specs.append(pl.BlockSpec((1, d), lambda i: (0, 0)))
        args.append(final_g.reshape(1, d))
    return pl.pallas_call(
        functools.partial(_mlp_kernel, layer=layer, tf=tf, n_chunks=n_chunks, final_norm=final_norm),
        grid=(m // tm,),
        in_specs=in_specs,
        out_specs=pl.BlockSpec((tm, d), lambda i: (i, 0)),
        out_shape=jax.ShapeDtypeStruct((m, d), _F32),
        scratch_shapes=[
            pltpu.VMEM((tm, d), _BF16),
            pltpu.VMEM((2, d, tf), w_up.dtype),
            pltpu.VMEM((2, tf, d), w_down.dtype),
            pltpu.SemaphoreType.DMA((2, 2)),
        ],
        compiler_params=pltpu.CompilerParams(
            dimension_semantics=("arbitrary",),
            vmem_limit_bytes=_VMEM_LIMIT_BYTES),
    )(*args)


def _proj_kernel(x_ref, g_ref, w_ref, o_ref, *, q_part, q_scale):
    x = x_ref[...]
    xn = (x * _inv_rms(x) * g_ref[0]).astype(_BF16)
    y = jnp.dot(xn, w_ref[...], preferred_element_type=_F32)
    y = y * jnp.where(pl.program_id(1) == q_part, q_scale, 1.0)
    for h in range(o_ref.shape[0]):
        o_ref[h] = y[:, h * _HEAD_DIM:(h + 1) * _HEAD_DIM].astype(_BF16)


def _kvq_proj(x2d, gains, w_cat, q_scale, n_heads, *, tm=1024):
    m, d = x2d.shape
    n_parts = gains.shape[0]
    assert w_cat.shape == (d, n_parts * d) and d == n_heads * _HEAD_DIM
    return pl.pallas_call(
        functools.partial(_proj_kernel, q_part=n_parts - 1, q_scale=q_scale),
        grid=(m // tm, n_parts),
        in_specs=[
            pl.BlockSpec((tm, d), lambda i, n: (i, 0)),
            pl.BlockSpec((1, 1, d), lambda i, n: (n, 0, 0)),
            pl.BlockSpec((d, d), lambda i, n: (0, n)),
        ],
        out_specs=pl.BlockSpec((n_heads, tm, _HEAD_DIM), lambda i, n: (n, i, 0)),
        out_shape=jax.ShapeDtypeStruct((n_parts * n_heads, m, _HEAD_DIM), _BF16),
        compiler_params=pltpu.CompilerParams(
            dimension_semantics=("parallel", "arbitrary"),
            vmem_limit_bytes=_VMEM_LIMIT_BYTES),
    )(x2d, gains, w_cat)


def _log2_sigmoid(z2):
    return jnp.minimum(z2, 0.0) - jnp.log2(1.0 + jnp.exp2(-jnp.abs(z2)))


def _split_hi_lo(x):
    hi = x.astype(_BF16)
    lo = (x - hi.astype(_F32)).astype(_BF16)
    return jnp.concatenate([hi, lo], axis=1)


def _attn_kernel(q_ref, k_ref, v_ref, o_ref, lb_ref, lhs_ref, cs_ref, reach_ref, *, seq):
    t = _ATTN_BLOCK
    n_blocks = seq // t
    row = lax.broadcasted_iota(jnp.int32, (t, t), 0)
    col = lax.broadcasted_iota(jnp.int32, (t, t), 1)
    strictly_causal = col < row
    later = jnp.where(row > col, 1.0, 0.0).astype(_BF16)
    ones = jnp.ones((t, t), _BF16)
    sums = jnp.concatenate([jnp.concatenate([later, ones], axis=1)] * 2, axis=0)
    nt_dims = (((1,), (1,)), ((), ()))

    n_win = _ATTN_BACK_BLOCKS + 1

    def logs_phase(i, q0, n_back):
        w0 = q0 - n_back * t
        z_all = lax.dot_general(q_ref[0, pl.ds(q0, t), :], k_ref[0, pl.ds(w0, (n_back + 1) * t), :],
                                nt_dims, preferred_element_type=_F32)
        for jb in range(n_back + 1):
            z2 = z_all[:, (n_back - jb) * t:(n_back - jb + 1) * t]
            lb = _log2_sigmoid(z2)
            l1 = lb - z2
            if jb == 0:
                l1 = jnp.where(strictly_causal, l1, 0.0)
            lb_ref[i, :, jb * t:(jb + 1) * t] = lb
            lhs_ref[pl.ds((i * n_win + jb) * t, t), :] = _split_hi_lo(l1)
        for jb in range(n_back + 1, n_win):
            lhs_ref[pl.ds((i * n_win + jb) * t, t), :] = jnp.zeros((t, 2 * t), _BF16)

    def weights_phase(i, q0, n_back):
        w0 = q0 - n_back * t
        carry = None
        weights = []
        for jb in range(n_back + 1):
            rows = pl.ds((i * n_win + jb) * t, t)
            between = cs_ref[rows, :t] if carry is None else cs_ref[rows, :t] + carry
            a = jnp.exp2(lb_ref[i, :, jb * t:(jb + 1) * t] + between)
            if jb == 0:
                a = jnp.where(strictly_causal, a, 0.0)
            weights.append(a.astype(_BF16))
            carry = cs_ref[rows, t:] if carry is None else carry + cs_ref[rows, t:]
        a_all = jnp.concatenate(weights[::-1], axis=1)
        out = jnp.dot(a_all, v_ref[0, pl.ds(w0, (n_back + 1) * t), :], preferred_element_type=_F32)
        o_ref[0, pl.ds(q0, t), :] = out.astype(o_ref.dtype)
        reach_ref[i] = jnp.max(carry)

    def sums_phase(blocks):
        rows = slice(blocks[0] * n_win * t, (blocks[-1] + 1) * n_win * t)
        cs_ref[rows, :] = jnp.dot(lhs_ref[rows, :], sums, preferred_element_type=_F32)

    groups = [list(range(g, min(g + _ATTN_GROUP, n_blocks))) for g in range(0, n_blocks, _ATTN_GROUP)]
    for step in range(len(groups) + 2):
        if step >= 2:
            for i in groups[step - 2]:
                weights_phase(i, i * t, min(i, _ATTN_BACK_BLOCKS))
        if 1 <= step <= len(groups):
            sums_phase(groups[step - 1])
        if step < len(groups):
            for i in groups[step]:
                logs_phase(i, i * t, min(i, _ATTN_BACK_BLOCKS))

    def continue_block(i, _):
        @pl.when(reach_ref[i] > _LOG2_WEIGHT_FLOOR)
        def _():
            q0 = pl.multiple_of(i * t, t)
            q = q_ref[0, pl.ds(q0, t), :]
            carry = cs_ref[pl.ds(i * n_win * t, t), t:]
            for jb in range(1, n_win):
                carry = carry + cs_ref[pl.ds((i * n_win + jb) * t, t), t:]

            def cond(state):
                j, reach, _, _ = state
                return jnp.logical_and(j >= 0, reach > _LOG2_WEIGHT_FLOOR)

            def body(state):
                j, _, carry, acc = state
                k0 = pl.multiple_of(j * t, t)
                z2 = lax.dot_general(q, k_ref[0, pl.ds(k0, t), :], nt_dims, preferred_element_type=_F32)
                lb = _log2_sigmoid(z2)
                cs = jnp.dot(_split_hi_lo(lb - z2), sums, preferred_element_type=_F32)
                a = jnp.exp2(lb + cs[:, :t] + carry)
                acc = acc + jnp.dot(a.astype(_BF16), v_ref[0, pl.ds(k0, t), :],
                                    preferred_element_type=_F32)
                carry = carry + cs[:, t:]
                return j - 1, jnp.max(carry), carry, acc

            acc = o_ref[0, pl.ds(q0, t), :].astype(_F32)
            _, _, _, acc = lax.while_loop(cond, body, (i - n_win, reach_ref[i], carry, acc))
            o_ref[0, pl.ds(q0, t), :] = acc.astype(o_ref.dtype)

        return 0

    lax.fori_loop(n_win, n_blocks, continue_block, 0)


def _stick_breaking_attention(kvq, b, s, n_heads):
    t, n_win = _ATTN_BLOCK, _ATTN_BACK_BLOCKS + 1
    n_blocks = s // t
    assert s % t == 0 and n_blocks >= n_win
    return pl.pallas_call(
        functools.partial(_attn_kernel, seq=s),
        grid=(b, n_heads),
        in_specs=[
            pl.BlockSpec((1, s, _HEAD_DIM), lambda bi, h: (2 * n_heads + h, bi, 0)),
            pl.BlockSpec((1, s, _HEAD_DIM), lambda bi, h: (h, bi, 0)),
            pl.BlockSpec((1, s, _HEAD_DIM), lambda bi, h: (n_heads + h, bi, 0)),
        ],
        out_specs=pl.BlockSpec((1, s, _HEAD_DIM), lambda bi, h: (h, bi, 0)),
        out_shape=jax.ShapeDtypeStruct((n_heads, b * s, _HEAD_DIM), _BF16),
        scratch_shapes=[
            pltpu.VMEM((n_blocks, t, n_win * t), _F32),
            pltpu.VMEM((n_blocks * n_win * t, 2 * t), _BF16),
            pltpu.VMEM((n_blocks * n_win * t, 2 * t), _F32),
            pltpu.SMEM((n_blocks,), _F32),
        ],
        compiler_params=pltpu.CompilerParams(
            dimension_semantics=("parallel", "arbitrary"),
            vmem_limit_bytes=_VMEM_LIMIT_BYTES),
    )(kvq, kvq, kvq)


def _oproj_kernel(x_ref, o_ref_in, w_ref, out_ref):
    o = jnp.concatenate([o_ref_in[h] for h in range(o_ref_in.shape[0])], axis=1)
    out_ref[...] = x_ref[...] + jnp.dot(o, w_ref[...], preferred_element_type=_F32)


def _out_proj(x2d, o_heads, w_o, *, tm=512):
    m, d = x2d.shape
    n_heads = o_heads.shape[0]
    return pl.pallas_call(
        _oproj_kernel,
        grid=(m // tm,),
        in_specs=[
            pl.BlockSpec((tm, d), lambda i: (i, 0)),
            pl.BlockSpec((n_heads, tm, _HEAD_DIM), lambda i: (0, i, 0)),
            pl.BlockSpec((d, d), lambda i: (0, 0)),
        ],
        out_specs=pl.BlockSpec((tm, d), lambda i: (i, 0)),
        out_shape=jax.ShapeDtypeStruct((m, d), _F32),
        compiler_params=pltpu.CompilerParams(
            dimension_semantics=("parallel",),
            vmem_limit_bytes=_VMEM_LIMIT_BYTES),
    )(x2d, o_heads, w_o)


def kernel(x, pool_norm, pool_w, pool_scale, kv_norm, w_kv, attn_norm, w_q, w_o, mlp_norm,
           w_up, w_down, final_norm):
    b, s, d = x.shape
    n_heads = d // _HEAD_DIM
    m = b * s
    assert pool_norm.shape[0] == 1 and attn_norm.shape[0] == 1 and mlp_norm.shape[0] == 2

    x1 = _pool_mixer(x, pool_norm[0], pool_w[0].astype(_BF16), pool_scale[0])
    x2 = _mlp(x1.reshape(m, d), mlp_norm[0], w_up, w_down, 0)

    w_cat = jnp.concatenate([w_kv.astype(_BF16), w_q[0].astype(_BF16)], axis=1)
    q_scale = math.log2(math.e) / math.sqrt(_HEAD_DIM)
    gains = jnp.stack([kv_norm, kv_norm, attn_norm[0]]).reshape(3, 1, d)
    kvq = _kvq_proj(x2, gains, w_cat, q_scale, n_heads)
    o = _stick_breaking_attention(kvq, b, s, n_heads)
    x3 = _out_proj(x2, o, w_o[0].astype(_BF16))

    out = _mlp(x3, mlp_norm[1], w_up, w_down, 1, final_norm)
    return out.reshape(b, s, d)
```

```python
import functools
import math

import jax
import jax.numpy as jnp
from jax import lax
from jax.experimental import pallas as pl
from jax.experimental.pallas import tpu as pltpu

_EPS = 1e-6
_POOL_WINDOWS = (2, 4, 8, 16)
_HEAD_DIM = 128
_LANES = 128
_POOL_HALO = 16
_ATTN_BLOCK = 128
_ATTN_BACK_BLOCKS = 2
_ATTN_GROUP = 8
_LOG2_WEIGHT_FLOOR = -150.0
_VMEM_LIMIT_BYTES = 56 * 1024 * 1024

_F32 = jnp.float32
_BF16 = jnp.bfloat16


def _inv_rms(x):
    return lax.rsqrt(jnp.mean(x * x, axis=-1, keepdims=True) + _EPS)


def _pool_norm(raw, first_rows_of_sequence, pg):
    xn = raw * _inv_rms(raw) * pg
    is_halo = lax.broadcasted_iota(jnp.int32, (raw.shape[0], 1), 0) < _POOL_HALO
    return jnp.where(jnp.logical_and(first_rows_of_sequence, is_halo), 0.0, xn)


def _pool_diff(xn, pos, w, cols):
    hl = _POOL_HALO
    s = xn[:, cols]
    k = 1
    while k < w:
        s = s + pltpu.roll(s, k, axis=0)
        k *= 2
    inv_cnt = 1.0 / jnp.minimum(pos + 1, w).astype(_F32)
    return s[hl:] * inv_cnt - xn[hl:, cols]


def _pool_mlp_kernel(x_hbm, pg_ref, pw_ref, ps_ref, g_ref, wu_ref, wd_ref, o_ref, xn_ref, x_buf, x_sem,
                     *, tiles_per_seq, slice_rows):
    i = pl.program_id(0)
    f = pl.program_id(1)
    n_f = pl.num_programs(1)
    hl = _POOL_HALO
    tm = o_ref.shape[0]
    n_slices = tm // slice_rows
    first_pool_step = n_f - n_slices
    group_dim = pw_ref.shape[1]

    def x_copies(tile):
        start = pl.multiple_of(tile * tm, tm)
        halo_start = pl.multiple_of(jnp.maximum(start - hl, 0), hl)
        return (
            pltpu.make_async_copy(x_hbm.at[pl.ds(start, tm), :], x_buf.at[pl.ds(hl, tm), :], x_sem.at[0]),
            pltpu.make_async_copy(x_hbm.at[pl.ds(halo_start, hl), :], x_buf.at[pl.ds(0, hl), :], x_sem.at[1]),
        )

    def pool_slice(tile, s):
        r0 = s * slice_rows if isinstance(s, int) else pl.multiple_of(s * slice_rows, slice_rows)
        seq_tile = tile % tiles_per_seq
        half = group_dim // 2
        st = {}

        def load_and_norm():
            st["raw"] = x_buf[pl.ds(r0, hl + slice_rows), :]
            st["xn"] = _pool_norm(st["raw"], jnp.logical_and(seq_tile == 0, s == 0), pg_ref[...])
            st["pos"] = seq_tile * tm + r0 + lax.broadcasted_iota(jnp.int32, (slice_rows, 1), 0)

        def diff_half(gi, hf):
            c0 = gi * group_dim + hf * half
            return _pool_diff(st["xn"], st["pos"], _POOL_WINDOWS[gi], slice(c0, c0 + half))

        def first_half(gi):
            st["diff"] = diff_half(gi, 0)

        def finish_group(gi):
            cols = slice(gi * group_dim, (gi + 1) * group_dim)
            diff = jnp.concatenate([st["diff"], diff_half(gi, 1)], axis=1)
            y = jnp.dot(diff.astype(_BF16), pw_ref[gi], preferred_element_type=_F32)
            x_buf[pl.ds(hl + r0, slice_rows), cols] = st["raw"][hl:, cols] + y * ps_ref[:, cols]

        pieces = [load_and_norm]
        for gi in range(len(_POOL_WINDOWS)):
            pieces += [functools.partial(first_half, gi), functools.partial(finish_group, gi)]
        return pieces

    @pl.when(jnp.logical_and(i == 0, f == 0))
    def _():
        for cp in x_copies(0):
            cp.start()
        for cp in x_copies(0):
            cp.wait()
        for s in reversed(range(n_slices)):
            for piece in pool_slice(0, s):
                piece()

    has_next = i + 1 < pl.num_programs(0)

    @pl.when(jnp.logical_and(f == 1, has_next))
    def _():
        for cp in x_copies(i + 1):
            cp.start()

    @pl.when(jnp.logical_and(f == first_pool_step - 1, has_next))
    def _():
        for cp in x_copies(i + 1):
            cp.wait()

    def chunk(xn, base):
        h = jnp.dot(xn, wu_ref[0].astype(_BF16), preferred_element_type=_F32)
        h = jnp.maximum(h, 0.0)
        h = (h * h).astype(_BF16)
        o_ref[...] = base + jnp.dot(h, wd_ref[0].astype(_BF16), preferred_element_type=_F32)

    @pl.when(f == 0)
    def _():
        x1 = x_buf[pl.ds(hl, tm), :]
        xn = (x1 * _inv_rms(x1) * g_ref[...]).astype(_BF16)
        xn_ref[...] = xn
        chunk(xn, x1)

    @pl.when(jnp.logical_and(f > 0, f < first_pool_step))
    def _():
        chunk(xn_ref[...], o_ref[...])

    @pl.when(f >= first_pool_step)
    def _():
        pool_pieces = pool_slice(i + 1, n_f - 1 - f)
        width = 2 * _LANES
        xn = xn_ref[...]
        hs = []
        for c in range(0, wu_ref.shape[2], width):
            hp = jnp.dot(xn, wu_ref[0, :, c:c + width].astype(_BF16), preferred_element_type=_F32)
            hp = jnp.maximum(hp, 0.0)
            hs.append((hp * hp).astype(_BF16))
            if pool_pieces:
                pool_pieces.pop(0)()
        h = jnp.concatenate(hs, axis=1)
        for c in range(0, o_ref.shape[1], width):
            o_ref[:, c:c + width] = o_ref[:, c:c + width] + jnp.dot(
                h, wd_ref[0, :, c:c + width].astype(_BF16), preferred_element_type=_F32)
            if pool_pieces:
                pool_pieces.pop(0)()
        for piece in pool_pieces:
            piece()


def _pool_mlp(x, pool_g, pool_w_bf16, pool_scale, norm_g, w_up, w_down, layer, *, tm=1024, tf=512,
              slice_rows=256):
    b, s, d = x.shape
    m = b * s
    d_ff = w_up.shape[2]
    n_groups = len(_POOL_WINDOWS)
    group_dim = d // n_groups
    n_slices = tm // slice_rows
    assert s % tm == 0 and tm % slice_rows == 0 and slice_rows % _POOL_HALO == 0
    assert d_ff // tf - n_slices >= 3
    return pl.pallas_call(
        functools.partial(_pool_mlp_kernel, tiles_per_seq=s // tm, slice_rows=slice_rows),
        grid=(m // tm, d_ff // tf),
        in_specs=[
            pl.BlockSpec(memory_space=pl.ANY),
            pl.BlockSpec((1, d), lambda i, f: (0, 0)),
            pl.BlockSpec((n_groups, group_dim, group_dim), lambda i, f: (0, 0, 0)),
            pl.BlockSpec((1, d), lambda i, f: (0, 0)),
            pl.BlockSpec((1, d), lambda i, f: (0, 0)),
            pl.BlockSpec((1, d, tf), lambda i, f: (layer, 0, f)),
            pl.BlockSpec((1, tf, d), lambda i, f: (layer, f, 0)),
        ],
        out_specs=pl.BlockSpec((tm, d), lambda i, f: (i, 0)),
        out_shape=jax.ShapeDtypeStruct((m, d), _F32),
        scratch_shapes=[
            pltpu.VMEM((tm, d), _BF16),
            pltpu.VMEM((_POOL_HALO + tm, d), _F32),
            pltpu.SemaphoreType.DMA((2,)),
        ],
        compiler_params=pltpu.CompilerParams(
            dimension_semantics=("arbitrary", "arbitrary"),
            vmem_limit_bytes=_VMEM_LIMIT_BYTES),
    )(x.reshape(m, d), pool_g.reshape(1, d), pool_w_bf16, pool_scale.reshape(1, d), norm_g.reshape(1, d),
      w_up, w_down)


def _mlp_kernel(x_ref, g_ref, wu_ref, wd_ref, *rest, final_norm):
    if final_norm:
        gf_ref, o_ref, xn_ref = rest
    else:
        o_ref, xn_ref = rest
    f = pl.program_id(1)

    def chunk(xn, base):
        h = jnp.dot(xn, wu_ref[0].astype(_BF16), preferred_element_type=_F32)
        h = jnp.maximum(h, 0.0)
        h = (h * h).astype(_BF16)
        o_ref[...] = base + jnp.dot(h, wd_ref[0].astype(_BF16), preferred_element_type=_F32)

    @pl.when(f == 0)
    def _():
        x = x_ref[...]
        xn = (x * _inv_rms(x) * g_ref[...]).astype(_BF16)
        xn_ref[...] = xn
        chunk(xn, x)

    @pl.when(f > 0)
    def _():
        chunk(xn_ref[...], o_ref[...])

    if final_norm:
        @pl.when(f == pl.num_programs(1) - 1)
        def _():
            r = o_ref[...]
            o_ref[...] = r * _inv_rms(r) * gf_ref[...]


def _mlp(x2d, norm_g, w_up, w_down, layer, final_g=None, *, tm=1024, tf=512):
    m, d = x2d.shape
    d_ff = w_up.shape[2]
    final_norm = final_g is not None
    in_specs = [
        pl.BlockSpec((tm, d), lambda i, f: (i, 0)),
        pl.BlockSpec((1, d), lambda i, f: (0, 0)),
        pl.BlockSpec((1, d, tf), lambda i, f: (layer, 0, f)),
        pl.BlockSpec((1, tf, d), lambda i, f: (layer, f, 0)),
    ]
    args = [x2d, norm_g.reshape(1, d), w_up, w_down]
    if final_norm:
        in_specs.append(pl.BlockSpec((1, d), lambda i, f: (0, 0)))
        args.append(final_g.reshape(1, d))
    return pl.pallas_call(
        functools.partial(_mlp_kernel, final_norm=final_norm),
        grid=(m // tm, d_ff // tf),
        in_specs=in_specs,
        out_specs=pl.BlockSpec((tm, d), lambda i, f: (i, 0)),
        out_shape=jax.ShapeDtypeStruct((m, d), _F32),
        scratch_shapes=[pltpu.VMEM((tm, d), _BF16)],
        compiler_params=pltpu.CompilerParams(
            dimension_semantics=("parallel", "arbitrary"),
            vmem_limit_bytes=_VMEM_LIMIT_BYTES),
    )(*args)


def _proj_kernel(x_ref, g_ref, w_ref, o_ref, inv_ref, *, q_part, q_scale):
    part = pl.program_id(1)

    @pl.when(part == 0)
    def _():
        inv_ref[...] = jnp.broadcast_to(_inv_rms(x_ref[...]), inv_ref.shape)

    x = x_ref[...]
    inv = inv_ref[...]
    lanes = inv.shape[1]
    xh = jnp.concatenate([x[:, c:c + lanes] * inv for c in range(0, x.shape[1], lanes)], axis=1)
    xn = (xh * g_ref[0]).astype(_BF16)
    y = jnp.dot(xn, w_ref[...], preferred_element_type=_F32)
    y = y * jnp.where(part == q_part, q_scale, 1.0)
    for h in range(o_ref.shape[0]):
        o_ref[h] = y[:, h * _HEAD_DIM:(h + 1) * _HEAD_DIM].astype(_BF16)


def _kvq_proj(x2d, gains, w_cat, q_scale, n_heads, *, tm=1024):
    m, d = x2d.shape
    n_parts = gains.shape[0]
    assert w_cat.shape == (d, n_parts * d) and d == n_heads * _HEAD_DIM
    return pl.pallas_call(
        functools.partial(_proj_kernel, q_part=n_parts - 1, q_scale=q_scale),
        grid=(m // tm, n_parts),
        in_specs=[
            pl.BlockSpec((tm, d), lambda i, n: (i, 0)),
            pl.BlockSpec((1, 1, d), lambda i, n: (n, 0, 0)),
            pl.BlockSpec((d, d), lambda i, n: (0, n)),
        ],
        out_specs=pl.BlockSpec((n_heads, tm, _HEAD_DIM), lambda i, n: (n, i, 0)),
        out_shape=jax.ShapeDtypeStruct((n_parts * n_heads, m, _HEAD_DIM), _BF16),
        scratch_shapes=[pltpu.VMEM((tm, _LANES), _F32)],
        compiler_params=pltpu.CompilerParams(
            dimension_semantics=("parallel", "arbitrary"),
            vmem_limit_bytes=_VMEM_LIMIT_BYTES),
    )(x2d, gains, w_cat)


def _log2_sigmoid(z2):
    return jnp.minimum(z2, 0.0) - jnp.log2(1.0 + jnp.exp2(-jnp.abs(z2)))


def _split_hi_lo(x):
    hi = x.astype(_BF16)
    lo = (x - hi.astype(_F32)).astype(_BF16)
    return jnp.concatenate([hi, lo], axis=1)


def _attn_kernel(q_ref, k_ref, v_ref, o_ref, lb_ref, lhs_ref, cs_ref, reach_ref, *, seq):
    t = _ATTN_BLOCK
    n_blocks = seq // t
    row = lax.broadcasted_iota(jnp.int32, (t, t), 0)
    col = lax.broadcasted_iota(jnp.int32, (t, t), 1)
    strictly_causal = col < row
    later = jnp.where(row > col, 1.0, 0.0).astype(_BF16)
    ones = jnp.ones((t, t), _BF16)
    sums = jnp.concatenate([jnp.concatenate([later, ones], axis=1)] * 2, axis=0)
    nt_dims = (((1,), (1,)), ((), ()))

    n_win = _ATTN_BACK_BLOCKS + 1

    def logs_phase(i, q0, n_back):
        w0 = q0 - n_back * t
        z_all = lax.dot_general(q_ref[0, pl.ds(q0, t), :], k_ref[0, pl.ds(w0, (n_back + 1) * t), :],
                                nt_dims, preferred_element_type=_F32)
        for jb in range(n_back + 1):
            z2 = z_all[:, (n_back - jb) * t:(n_back - jb + 1) * t]
            lb = _log2_sigmoid(z2)
            l1 = lb - z2
            if jb == 0:
                l1 = jnp.where(strictly_causal, l1, 0.0)
            lb_ref[i, :, jb * t:(jb + 1) * t] = lb
            lhs_ref[pl.ds((i * n_win + jb) * t, t), :] = _split_hi_lo(l1)
        for jb in range(n_back + 1, n_win):
            lhs_ref[pl.ds((i * n_win + jb) * t, t), :] = jnp.zeros((t, 2 * t), _BF16)

    def weights_phase(i, q0, n_back):
        w0 = q0 - n_back * t
        carry = None
        weights = []
        for jb in range(n_back + 1):
            rows = pl.ds((i * n_win + jb) * t, t)
            between = cs_ref[rows, :t] if carry is None else cs_ref[rows, :t] + carry
            a = jnp.exp2(lb_ref[i, :, jb * t:(jb + 1) * t] + between)
            if jb == 0:
                a = jnp.where(strictly_causal, a, 0.0)
            weights.append(a.astype(_BF16))
            carry = cs_ref[rows, t:] if carry is None else carry + cs_ref[rows, t:]
        a_all = jnp.concatenate(weights[::-1], axis=1)
        out = jnp.dot(a_all, v_ref[0, pl.ds(w0, (n_back + 1) * t), :], preferred_element_type=_F32)
        o_ref[0, pl.ds(q0, t), :] = out.astype(o_ref.dtype)
        reach_ref[i] = jnp.max(carry)

    def sums_phase(blocks):
        rows = slice(blocks[0] * n_win * t, (blocks[-1] + 1) * n_win * t)
        cs_ref[rows, :] = jnp.dot(lhs_ref[rows, :], sums, preferred_element_type=_F32)

    groups = [list(range(g, min(g + _ATTN_GROUP, n_blocks))) for g in range(0, n_blocks, _ATTN_GROUP)]
    for step in range(len(groups) + 2):
        if step >= 2:
            for i in groups[step - 2]:
                weights_phase(i, i * t, min(i, _ATTN_BACK_BLOCKS))
        if 1 <= step <= len(groups):
            sums_phase(groups[step - 1])
        if step < len(groups):
            for i in groups[step]:
                logs_phase(i, i * t, min(i, _ATTN_BACK_BLOCKS))

    def continue_block(i, _):
        @pl.when(reach_ref[i] > _LOG2_WEIGHT_FLOOR)
        def _():
            q0 = pl.multiple_of(i * t, t)
            q = q_ref[0, pl.ds(q0, t), :]
            carry = cs_ref[pl.ds(i * n_win * t, t), t:]
            for jb in range(1, n_win):
                carry = carry + cs_ref[pl.ds((i * n_win + jb) * t, t), t:]

            def cond(state):
                j, reach, _, _ = state
                return jnp.logical_and(j >= 0, reach > _LOG2_WEIGHT_FLOOR)

            def body(state):
                j, _, carry, acc = state
                k0 = pl.multiple_of(j * t, t)
                z2 = lax.dot_general(q, k_ref[0, pl.ds(k0, t), :], nt_dims, preferred_element_type=_F32)
                lb = _log2_sigmoid(z2)
                cs = jnp.dot(_split_hi_lo(lb - z2), sums, preferred_element_type=_F32)
                a = jnp.exp2(lb + cs[:, :t] + carry)
                acc = acc + jnp.dot(a.astype(_BF16), v_ref[0, pl.ds(k0, t), :],
                                    preferred_element_type=_F32)
                carry = carry + cs[:, t:]
                return j - 1, jnp.max(carry), carry, acc

            acc = o_ref[0, pl.ds(q0, t), :].astype(_F32)
            _, _, _, acc = lax.while_loop(cond, body, (i - n_win, reach_ref[i], carry, acc))
            o_ref[0, pl.ds(q0, t), :] = acc.astype(o_ref.dtype)

        return 0

    lax.fori_loop(n_win, n_blocks, continue_block, 0)


def _stick_breaking_attention(kvq, b, s, n_heads):
    t, n_win = _ATTN_BLOCK, _ATTN_BACK_BLOCKS + 1
    n_blocks = s // t
    assert s % t == 0 and n_blocks >= n_win
    return pl.pallas_call(
        functools.partial(_attn_kernel, seq=s),
        grid=(b, n_heads),
        in_specs=[
            pl.BlockSpec((1, s, _HEAD_DIM), lambda bi, h: (2 * n_heads + h, bi, 0)),
            pl.BlockSpec((1, s, _HEAD_DIM), lambda bi, h: (h, bi, 0)),
            pl.BlockSpec((1, s, _HEAD_DIM), lambda bi, h: (n_heads + h, bi, 0)),
        ],
        out_specs=pl.BlockSpec((1, s, _HEAD_DIM), lambda bi, h: (h, bi, 0)),
        out_shape=jax.ShapeDtypeStruct((n_heads, b * s, _HEAD_DIM), _BF16),
        scratch_shapes=[
            pltpu.VMEM((n_blocks, t, n_win * t), _F32),
            pltpu.VMEM((n_blocks * n_win * t, 2 * t), _BF16),
            pltpu.VMEM((n_blocks * n_win * t, 2 * t), _F32),
            pltpu.SMEM((n_blocks,), _F32),
        ],
        compiler_params=pltpu.CompilerParams(
            dimension_semantics=("parallel", "arbitrary"),
            vmem_limit_bytes=_VMEM_LIMIT_BYTES),
    )(kvq, kvq, kvq)


def _oproj_kernel(x_ref, o_ref_in, w_ref, out_ref):
    o = jnp.concatenate([o_ref_in[h] for h in range(o_ref_in.shape[0])], axis=1)
    out_ref[...] = x_ref[...] + jnp.dot(o, w_ref[...].astype(_BF16), preferred_element_type=_F32)


def _out_proj(x2d, o_heads, w_o, *, tm=512):
    m, d = x2d.shape
    n_heads = o_heads.shape[0]
    return pl.pallas_call(
        _oproj_kernel,
        grid=(m // tm,),
        in_specs=[
            pl.BlockSpec((tm, d), lambda i: (i, 0)),
            pl.BlockSpec((n_heads, tm, _HEAD_DIM), lambda i: (0, i, 0)),
            pl.BlockSpec((d, d), lambda i: (0, 0)),
        ],
        out_specs=pl.BlockSpec((tm, d), lambda i: (i, 0)),
        out_shape=jax.ShapeDtypeStruct((m, d), _F32),
        compiler_params=pltpu.CompilerParams(
            dimension_semantics=("parallel",),
            vmem_limit_bytes=_VMEM_LIMIT_BYTES),
    )(x2d, o_heads, w_o)


def kernel(x, pool_norm, pool_w, pool_scale, kv_norm, w_kv, attn_norm, w_q, w_o, mlp_norm,
           w_up, w_down, final_norm):
    b, s, d = x.shape
    n_heads = d // _HEAD_DIM
    m = b * s
    assert pool_norm.shape[0] == 1 and attn_norm.shape[0] == 1 and mlp_norm.shape[0] == 2

    x2 = _pool_mlp(x, pool_norm[0], pool_w[0].astype(_BF16), pool_scale[0], mlp_norm[0], w_up, w_down, 0)

    w_cat = jnp.concatenate([w_kv.astype(_BF16), w_q[0].astype(_BF16)], axis=1)
    q_scale = math.log2(math.e) / math.sqrt(_HEAD_DIM)
    gains = jnp.stack([kv_norm, kv_norm, attn_norm[0]]).reshape(3, 1, d)
    kvq = _kvq_proj(x2, gains, w_cat, q_scale, n_heads)
    o = _stick_breaking_attention(kvq, b, s, n_heads)
    x3 = _out_proj(x2, o, w_o[0])

    out = _mlp(x3, mlp_norm[1], w_up, w_down, 1, final_norm)
    return out.reshape(b, s, d)
```

```python
import functools
import math

import jax
import jax.numpy as jnp
from jax import lax
from jax.experimental import pallas as pl
from jax.experimental.pallas import tpu as pltpu

_EPS = 1e-6
_POOL_WINDOWS = (2, 4, 8, 16)
_HEAD_DIM = 128
_LANES = 128
_POOL_HALO = 16
_ATTN_BLOCK = 128
_ATTN_BACK_BLOCKS = 2
_ATTN_GROUP = 8
_LOG2_WEIGHT_FLOOR = -150.0
_VMEM_LIMIT_BYTES = 56 * 1024 * 1024

_F32 = jnp.float32
_BF16 = jnp.bfloat16


def _inv_rms(x):
    return lax.rsqrt(jnp.mean(x * x, axis=-1, keepdims=True) + _EPS)


def _pool_norm(raw, first_rows_of_sequence, pg):
    xn = raw * _inv_rms(raw) * pg
    is_halo = lax.broadcasted_iota(jnp.int32, (raw.shape[0], 1), 0) < _POOL_HALO
    return jnp.where(jnp.logical_and(first_rows_of_sequence, is_halo), 0.0, xn)


def _pool_diff(xn, pos, w, cols):
    hl = _POOL_HALO
    s = xn[:, cols]
    k = 1
    while k < w:
        s = s + pltpu.roll(s, k, axis=0)
        k *= 2
    inv_cnt = 1.0 / jnp.minimum(pos + 1, w).astype(_F32)
    return s[hl:] * inv_cnt - xn[hl:, cols]


def _pool_mlp_kernel(x_hbm, pg_ref, pw_ref, ps_ref, g_ref, wu_ref, wd_ref, o_ref, xn_ref, x_buf, x_sem,
                     *, tiles_per_seq, slice_rows):
    i = pl.program_id(0)
    f = pl.program_id(1)
    n_f = pl.num_programs(1)
    hl = _POOL_HALO
    tm = o_ref.shape[0]
    n_slices = tm // slice_rows
    first_pool_step = n_f - n_slices
    group_dim = pw_ref.shape[1]

    def x_copies(tile):
        start = pl.multiple_of(tile * tm, tm)
        halo_start = pl.multiple_of(jnp.maximum(start - hl, 0), hl)
        return (
            pltpu.make_async_copy(x_hbm.at[pl.ds(start, tm), :], x_buf.at[pl.ds(hl, tm), :], x_sem.at[0]),
            pltpu.make_async_copy(x_hbm.at[pl.ds(halo_start, hl), :], x_buf.at[pl.ds(0, hl), :], x_sem.at[1]),
        )

    def pool_slice(tile, s):
        r0 = s * slice_rows if isinstance(s, int) else pl.multiple_of(s * slice_rows, slice_rows)
        seq_tile = tile % tiles_per_seq
        half = group_dim // 2
        st = {}

        def load_and_norm():
            st["raw"] = x_buf[pl.ds(r0, hl + slice_rows), :]
            st["xn"] = _pool_norm(st["raw"], jnp.logical_and(seq_tile == 0, s == 0), pg_ref[...])
            st["pos"] = seq_tile * tm + r0 + lax.broadcasted_iota(jnp.int32, (slice_rows, 1), 0)

        def diff_half(gi, hf):
            c0 = gi * group_dim + hf * half
            return _pool_diff(st["xn"], st["pos"], _POOL_WINDOWS[gi], slice(c0, c0 + half))

        def first_half(gi):
            st["diff"] = diff_half(gi, 0)

        def finish_group(gi):
            cols = slice(gi * group_dim, (gi + 1) * group_dim)
            diff = jnp.concatenate([st["diff"], diff_half(gi, 1)], axis=1)
            y = jnp.dot(diff.astype(_BF16), pw_ref[gi], preferred_element_type=_F32)
            x_buf[pl.ds(hl + r0, slice_rows), cols] = st["raw"][hl:, cols] + y * ps_ref[:, cols]

        pieces = [load_and_norm]
        for gi in range(len(_POOL_WINDOWS)):
            pieces += [functools.partial(first_half, gi), functools.partial(finish_group, gi)]
        return pieces

    @pl.when(jnp.logical_and(i == 0, f == 0))
    def _():
        for cp in x_copies(0):
            cp.start()
        for cp in x_copies(0):
            cp.wait()
        for s in reversed(range(n_slices)):
            for piece in pool_slice(0, s):
                piece()

    has_next = i + 1 < pl.num_programs(0)

    @pl.when(jnp.logical_and(f == 1, has_next))
    def _():
        for cp in x_copies(i + 1):
            cp.start()

    @pl.when(jnp.logical_and(f == first_pool_step - 1, has_next))
    def _():
        for cp in x_copies(i + 1):
            cp.wait()

    def chunk(xn, base):
        h = jnp.dot(xn, wu_ref[0].astype(_BF16), preferred_element_type=_F32)
        h = jnp.maximum(h, 0.0)
        h = (h * h).astype(_BF16)
        o_ref[...] = base + jnp.dot(h, wd_ref[0].astype(_BF16), preferred_element_type=_F32)

    @pl.when(f == 0)
    def _():
        x1 = x_buf[pl.ds(hl, tm), :]
        xn = (x1 * _inv_rms(x1) * g_ref[...]).astype(_BF16)
        xn_ref[...] = xn
        chunk(xn, x1)

    @pl.when(jnp.logical_and(f > 0, f < first_pool_step))
    def _():
        chunk(xn_ref[...], o_ref[...])

    @pl.when(f >= first_pool_step)
    def _():
        pool_pieces = pool_slice(i + 1, n_f - 1 - f)
        width = 2 * _LANES
        xn = xn_ref[...]
        hs = []
        for c in range(0, wu_ref.shape[2], width):
            hp = jnp.dot(xn, wu_ref[0, :, c:c + width].astype(_BF16), preferred_element_type=_F32)
            hp = jnp.maximum(hp, 0.0)
            hs.append((hp * hp).astype(_BF16))
            if pool_pieces:
                pool_pieces.pop(0)()
        h = jnp.concatenate(hs, axis=1)
        for c in range(0, o_ref.shape[1], width):
            o_ref[:, c:c + width] = o_ref[:, c:c + width] + jnp.dot(
                h, wd_ref[0, :, c:c + width].astype(_BF16), preferred_element_type=_F32)
            if pool_pieces:
                pool_pieces.pop(0)()
        for piece in pool_pieces:
            piece()


def _pool_mlp(x, pool_g, pool_w_bf16, pool_scale, norm_g, w_up, w_down, layer, *, tm=1024, tf=512,
              slice_rows=256):
    b, s, d = x.shape
    m = b * s
    d_ff = w_up.shape[2]
    n_groups = len(_POOL_WINDOWS)
    group_dim = d // n_groups
    n_slices = tm // slice_rows
    assert s % tm == 0 and tm % slice_rows == 0 and slice_rows % _POOL_HALO == 0
    assert d_ff // tf - n_slices >= 3
    return pl.pallas_call(
        functools.partial(_pool_mlp_kernel, tiles_per_seq=s // tm, slice_rows=slice_rows),
        grid=(m // tm, d_ff // tf),
        in_specs=[
            pl.BlockSpec(memory_space=pl.ANY),
            pl.BlockSpec((1, d), lambda i, f: (0, 0)),
            pl.BlockSpec((n_groups, group_dim, group_dim), lambda i, f: (0, 0, 0)),
            pl.BlockSpec((1, d), lambda i, f: (0, 0)),
            pl.BlockSpec((1, d), lambda i, f: (0, 0)),
            pl.BlockSpec((1, d, tf), lambda i, f: (layer, 0, f)),
            pl.BlockSpec((1, tf, d), lambda i, f: (layer, f, 0)),
        ],
        out_specs=pl.BlockSpec((tm, d), lambda i, f: (i, 0)),
        out_shape=jax.ShapeDtypeStruct((m, d), _F32),
        scratch_shapes=[
            pltpu.VMEM((tm, d), _BF16),
            pltpu.VMEM((_POOL_HALO + tm, d), _F32),
            pltpu.SemaphoreType.DMA((2,)),
        ],
        compiler_params=pltpu.CompilerParams(
            dimension_semantics=("arbitrary", "arbitrary"),
            vmem_limit_bytes=_VMEM_LIMIT_BYTES),
    )(x.reshape(m, d), pool_g.reshape(1, d), pool_w_bf16, pool_scale.reshape(1, d), norm_g.reshape(1, d),
      w_up, w_down)


def _mlp_kernel(x_ref, g_ref, wu_ref, wd_ref, *rest, final_norm):
    if final_norm:
        gf_ref, o_ref, xn_ref = rest
    else:
        o_ref, xn_ref = rest
    f = pl.program_id(1)

    def chunk(xn, base):
        h = jnp.dot(xn, wu_ref[0].astype(_BF16), preferred_element_type=_F32)
        h = jnp.maximum(h, 0.0)
        h = (h * h).astype(_BF16)
        o_ref[...] = base + jnp.dot(h, wd_ref[0].astype(_BF16), preferred_element_type=_F32)

    @pl.when(f == 0)
    def _():
        x = x_ref[...]
        xn = (x * _inv_rms(x) * g_ref[...]).astype(_BF16)
        xn_ref[...] = xn
        chunk(xn, x)

    @pl.when(f > 0)
    def _():
        chunk(xn_ref[...], o_ref[...])

    if final_norm:
        @pl.when(f == pl.num_programs(1) - 1)
        def _():
            r = o_ref[...]
            o_ref[...] = r * _inv_rms(r) * gf_ref[...]


def _mlp(x2d, norm_g, w_up, w_down, layer, final_g=None, *, tm=1024, tf=512):
    m, d = x2d.shape
    d_ff = w_up.shape[2]
    final_norm = final_g is not None
    in_specs = [
        pl.BlockSpec((tm, d), lambda i, f: (i, 0)),
        pl.BlockSpec((1, d), lambda i, f: (0, 0)),
        pl.BlockSpec((1, d, tf), lambda i, f: (layer, 0, f)),
        pl.BlockSpec((1, tf, d), lambda i, f: (layer, f, 0)),
    ]
    args = [x2d, norm_g.reshape(1, d), w_up, w_down]
    if final_norm:
        in_specs.append(pl.BlockSpec((1, d), lambda i, f: (0, 0)))
        args.append(final_g.reshape(1, d))
    return pl.pallas_call(
        functools.partial(_mlp_kernel, final_norm=final_norm),
        grid=(m // tm, d_ff // tf),
        in_specs=in_specs,
        out_specs=pl.BlockSpec((tm, d), lambda i, f: (i, 0)),
        out_shape=jax.ShapeDtypeStruct((m, d), _F32),
        scratch_shapes=[pltpu.VMEM((tm, d), _BF16)],
        compiler_params=pltpu.CompilerParams(
            dimension_semantics=("parallel", "arbitrary"),
            vmem_limit_bytes=_VMEM_LIMIT_BYTES),
    )(*args)


def _proj_kernel(x_ref, g_ref, wkv_ref, wq_ref, o_ref, inv_ref, *, q_part, q_scale):
    part = pl.program_id(1)

    @pl.when(part == 0)
    def _():
        inv_ref[...] = jnp.broadcast_to(_inv_rms(x_ref[...]), inv_ref.shape)

    x = x_ref[...]
    inv = inv_ref[...]
    lanes = inv.shape[1]
    xh = jnp.concatenate([x[:, c:c + lanes] * inv for c in range(0, x.shape[1], lanes)], axis=1)
    xn = (xh * g_ref[0]).astype(_BF16)
    w = jnp.where(part == q_part, wq_ref[...], wkv_ref[...])
    y = jnp.dot(xn, w, preferred_element_type=_F32)
    y = y * jnp.where(part == q_part, q_scale, 1.0)
    for h in range(o_ref.shape[0]):
        o_ref[h] = y[:, h * _HEAD_DIM:(h + 1) * _HEAD_DIM].astype(_BF16)


def _kvq_proj(x2d, gains, w_kv, w_q, q_scale, n_heads, *, tm=1024):
    m, d = x2d.shape
    n_parts = gains.shape[0]
    assert w_kv.shape == (d, (n_parts - 1) * d) and w_q.shape == (d, d) and d == n_heads * _HEAD_DIM
    return pl.pallas_call(
        functools.partial(_proj_kernel, q_part=n_parts - 1, q_scale=q_scale),
        grid=(m // tm, n_parts),
        in_specs=[
            pl.BlockSpec((tm, d), lambda i, n: (i, 0)),
            pl.BlockSpec((1, 1, d), lambda i, n: (n, 0, 0)),
            pl.BlockSpec((d, d), lambda i, n: (0, jnp.minimum(n, n_parts - 2))),
            pl.BlockSpec((d, d), lambda i, n: (0, 0), pipeline_mode=pl.Buffered(1)),
        ],
        out_specs=pl.BlockSpec((n_heads, tm, _HEAD_DIM), lambda i, n: (n, i, 0)),
        out_shape=jax.ShapeDtypeStruct((n_parts * n_heads, m, _HEAD_DIM), _BF16),
        scratch_shapes=[pltpu.VMEM((tm, _LANES), _F32)],
        compiler_params=pltpu.CompilerParams(
            dimension_semantics=("parallel", "arbitrary"),
            vmem_limit_bytes=_VMEM_LIMIT_BYTES),
    )(x2d, gains, w_kv, w_q)


def _log2_sigmoid(z2):
    return jnp.minimum(z2, 0.0) - jnp.log2(1.0 + jnp.exp2(-jnp.abs(z2)))


def _split_hi_lo(x):
    hi = x.astype(_BF16)
    lo = (x - hi.astype(_F32)).astype(_BF16)
    return jnp.concatenate([hi, lo], axis=1)


def _attn_kernel(q_ref, k_ref, v_ref, o_ref, lb_ref, lhs_ref, cs_ref, reach_ref, *, seq):
    t = _ATTN_BLOCK
    n_blocks = seq // t
    row = lax.broadcasted_iota(jnp.int32, (t, t), 0)
    col = lax.broadcasted_iota(jnp.int32, (t, t), 1)
    strictly_causal = col < row
    later = jnp.where(row > col, 1.0, 0.0).astype(_BF16)
    ones = jnp.ones((t, t), _BF16)
    sums = jnp.concatenate([jnp.concatenate([later, ones], axis=1)] * 2, axis=0)
    nt_dims = (((1,), (1,)), ((), ()))

    n_win = _ATTN_BACK_BLOCKS + 1

    def logs_phase(i, q0, n_back):
        w0 = q0 - n_back * t
        z_all = lax.dot_general(q_ref[0, pl.ds(q0, t), :], k_ref[0, pl.ds(w0, (n_back + 1) * t), :],
                                nt_dims, preferred_element_type=_F32)
        for jb in range(n_back + 1):
            z2 = z_all[:, (n_back - jb) * t:(n_back - jb + 1) * t]
            lb = _log2_sigmoid(z2)
            l1 = lb - z2
            if jb == 0:
                l1 = jnp.where(strictly_causal, l1, 0.0)
            lb_ref[i, :, jb * t:(jb + 1) * t] = lb
            lhs_ref[pl.ds((i * n_win + jb) * t, t), :] = _split_hi_lo(l1)
        for jb in range(n_back + 1, n_win):
            lhs_ref[pl.ds((i * n_win + jb) * t, t), :] = jnp.zeros((t, 2 * t), _BF16)

    def weights_phase(i, q0, n_back):
        w0 = q0 - n_back * t
        carry = None
        weights = []
        for jb in range(n_back + 1):
            rows = pl.ds((i * n_win + jb) * t, t)
            between = cs_ref[rows, :t] if carry is None else cs_ref[rows, :t] + carry
            a = jnp.exp2(lb_ref[i, :, jb * t:(jb + 1) * t] + between)
            if jb == 0:
                a = jnp.where(strictly_causal, a, 0.0)
            weights.append(a.astype(_BF16))
            carry = cs_ref[rows, t:] if carry is None else carry + cs_ref[rows, t:]
        a_all = jnp.concatenate(weights[::-1], axis=1)
        out = jnp.dot(a_all, v_ref[0, pl.ds(w0, (n_back + 1) * t), :], preferred_element_type=_F32)
        o_ref[0, pl.ds(q0, t), :] = out.astype(o_ref.dtype)
        reach_ref[i] = jnp.max(carry)

    def sums_phase(blocks):
        rows = slice(blocks[0] * n_win * t, (blocks[-1] + 1) * n_win * t)
        cs_ref[rows, :] = jnp.dot(lhs_ref[rows, :], sums, preferred_element_type=_F32)

    groups = [list(range(g, min(g + _ATTN_GROUP, n_blocks))) for g in range(0, n_blocks, _ATTN_GROUP)]
    for step in range(len(groups) + 2):
        if step >= 2:
            for i in groups[step - 2]:
                weights_phase(i, i * t, min(i, _ATTN_BACK_BLOCKS))
        if 1 <= step <= len(groups):
            sums_phase(groups[step - 1])
        if step < len(groups):
            for i in groups[step]:
                logs_phase(i, i * t, min(i, _ATTN_BACK_BLOCKS))

    def continue_block(i, _):
        @pl.when(reach_ref[i] > _LOG2_WEIGHT_FLOOR)
        def _():
            q0 = pl.multiple_of(i * t, t)
            q = q_ref[0, pl.ds(q0, t), :]
            carry = cs_ref[pl.ds(i * n_win * t, t), t:]
            for jb in range(1, n_win):
                carry = carry + cs_ref[pl.ds((i * n_win + jb) * t, t), t:]

            def cond(state):
                j, reach, _, _ = state
                return jnp.logical_and(j >= 0, reach > _LOG2_WEIGHT_FLOOR)

            def body(state):
                j, _, carry, acc = state
                k0 = pl.multiple_of(j * t, t)
                z2 = lax.dot_general(q, k_ref[0, pl.ds(k0, t), :], nt_dims, preferred_element_type=_F32)
                lb = _log2_sigmoid(z2)
                cs = jnp.dot(_split_hi_lo(lb - z2), sums, preferred_element_type=_F32)
                a = jnp.exp2(lb + cs[:, :t] + carry)
                acc = acc + jnp.dot(a.astype(_BF16), v_ref[0, pl.ds(k0, t), :],
                                    preferred_element_type=_F32)
                carry = carry + cs[:, t:]
                return j - 1, jnp.max(carry), carry, acc

            acc = o_ref[0, pl.ds(q0, t), :].astype(_F32)
            _, _, _, acc = lax.while_loop(cond, body, (i - n_win, reach_ref[i], carry, acc))
            o_ref[0, pl.ds(q0, t), :] = acc.astype(o_ref.dtype)

        return 0

    lax.fori_loop(n_win, n_blocks, continue_block, 0)


def _stick_breaking_attention(kvq, b, s, n_heads):
    t, n_win = _ATTN_BLOCK, _ATTN_BACK_BLOCKS + 1
    n_blocks = s // t
    assert s % t == 0 and n_blocks >= n_win
    return pl.pallas_call(
        functools.partial(_attn_kernel, seq=s),
        grid=(b, n_heads),
        in_specs=[
            pl.BlockSpec((1, s, _HEAD_DIM), lambda bi, h: (2 * n_heads + h, bi, 0)),
            pl.BlockSpec((1, s, _HEAD_DIM), lambda bi, h: (h, bi, 0)),
            pl.BlockSpec((1, s, _HEAD_DIM), lambda bi, h: (n_heads + h, bi, 0)),
        ],
        out_specs=pl.BlockSpec((1, s, _HEAD_DIM), lambda bi, h: (h, bi, 0)),
        out_shape=jax.ShapeDtypeStruct((n_heads, b * s, _HEAD_DIM), _BF16),
        scratch_shapes=[
            pltpu.VMEM((n_blocks, t, n_win * t), _F32),
            pltpu.VMEM((n_blocks * n_win * t, 2 * t), _BF16),
            pltpu.VMEM((n_blocks * n_win * t, 2 * t), _F32),
            pltpu.SMEM((n_blocks,), _F32),
        ],
        compiler_params=pltpu.CompilerParams(
            dimension_semantics=("parallel", "arbitrary"),
            vmem_limit_bytes=_VMEM_LIMIT_BYTES),
    )(kvq, kvq, kvq)


def _oproj_kernel(x_ref, o_ref_in, w_ref, out_ref):
    o = jnp.concatenate([o_ref_in[h] for h in range(o_ref_in.shape[0])], axis=1)
    out_ref[...] = x_ref[...] + jnp.dot(o, w_ref[...].astype(_BF16), preferred_element_type=_F32)


def _out_proj(x2d, o_heads, w_o, *, tm=512):
    m, d = x2d.shape
    n_heads = o_heads.shape[0]
    return pl.pallas_call(
        _oproj_kernel,
        grid=(m // tm,),
        in_specs=[
            pl.BlockSpec((tm, d), lambda i: (i, 0)),
            pl.BlockSpec((n_heads, tm, _HEAD_DIM), lambda i: (0, i, 0)),
            pl.BlockSpec((d, d), lambda i: (0, 0)),
        ],
        out_specs=pl.BlockSpec((tm, d), lambda i: (i, 0)),
        out_shape=jax.ShapeDtypeStruct((m, d), _F32),
        compiler_params=pltpu.CompilerParams(
            dimension_semantics=("parallel",),
            vmem_limit_bytes=_VMEM_LIMIT_BYTES),
    )(x2d, o_heads, w_o)


def kernel(x, pool_norm, pool_w, pool_scale, kv_norm, w_kv, attn_norm, w_q, w_o, mlp_norm,
           w_up, w_down, final_norm):
    b, s, d = x.shape
    n_heads = d // _HEAD_DIM
    m = b * s
    assert pool_norm.shape[0] == 1 and attn_norm.shape[0] == 1 and mlp_norm.shape[0] == 2

    x2 = _pool_mlp(x, pool_norm[0], pool_w[0].astype(_BF16), pool_scale[0], mlp_norm[0], w_up, w_down, 0)

    q_scale = math.log2(math.e) / math.sqrt(_HEAD_DIM)
    gains = jnp.stack([kv_norm, kv_norm, attn_norm[0]]).reshape(3, 1, d)
    kvq = _kvq_proj(x2, gains, w_kv.astype(_BF16), w_q[0].astype(_BF16), q_scale, n_heads)
    o = _stick_breaking_attention(kvq, b, s, n_heads)
    x3 = _out_proj(x2, o, w_o[0])

    out = _mlp(x3, mlp_norm[1], w_up, w_down, 1, final_norm)
    return out.reshape(b, s, d)
```

```python
import functools
import math

import jax
import jax.numpy as jnp
from jax import lax
from jax.experimental import pallas as pl
from jax.experimental.pallas import tpu as pltpu

_EPS = 1e-6
_POOL_WINDOWS = (2, 4, 8, 16)
_HEAD_DIM = 128
_LANES = 128
_POOL_HALO = 16
_ATTN_BLOCK = 128
_ATTN_BACK_BLOCKS = 2
_ATTN_GROUP = 16
_LOG2_WEIGHT_FLOOR = -150.0
_VMEM_LIMIT_BYTES = 56 * 1024 * 1024

_F32 = jnp.float32
_BF16 = jnp.bfloat16


def _inv_rms(x):
    return lax.rsqrt(jnp.mean(x * x, axis=-1, keepdims=True) + _EPS)


def _pool_norm(raw, first_rows_of_sequence, pg):
    xn = raw * _inv_rms(raw) * pg
    is_halo = lax.broadcasted_iota(jnp.int32, (raw.shape[0], 1), 0) < _POOL_HALO
    return jnp.where(jnp.logical_and(first_rows_of_sequence, is_halo), 0.0, xn)


def _pool_diff(xn, pos, w, cols):
    hl = _POOL_HALO
    s = xn[:, cols]
    k = 1
    while k < w:
        s = s + pltpu.roll(s, k, axis=0)
        k *= 2
    inv_cnt = 1.0 / jnp.minimum(pos + 1, w).astype(_F32)
    return s[hl:] * inv_cnt - xn[hl:, cols]


def _pool_mlp_kernel(x_hbm, pg_ref, pw_ref, ps_ref, g_ref, wu_ref, wd_ref, o_ref, xn_ref, x_buf, x_sem,
                     *, tiles_per_seq, slice_rows):
    i = pl.program_id(0)
    f = pl.program_id(1)
    n_f = pl.num_programs(1)
    hl = _POOL_HALO
    tm = o_ref.shape[0]
    n_slices = tm // slice_rows
    first_pool_step = n_f - n_slices
    group_dim = pw_ref.shape[1]

    def x_copies(tile):
        start = pl.multiple_of(tile * tm, tm)
        halo_start = pl.multiple_of(jnp.maximum(start - hl, 0), hl)
        return (
            pltpu.make_async_copy(x_hbm.at[pl.ds(start, tm), :], x_buf.at[pl.ds(hl, tm), :], x_sem.at[0]),
            pltpu.make_async_copy(x_hbm.at[pl.ds(halo_start, hl), :], x_buf.at[pl.ds(0, hl), :], x_sem.at[1]),
        )

    def pool_slice(tile, s):
        r0 = s * slice_rows if isinstance(s, int) else pl.multiple_of(s * slice_rows, slice_rows)
        seq_tile = tile % tiles_per_seq
        half = group_dim // 2
        st = {}

        def load_and_norm():
            st["raw"] = x_buf[pl.ds(r0, hl + slice_rows), :]
            st["xn"] = _pool_norm(st["raw"], jnp.logical_and(seq_tile == 0, s == 0), pg_ref[...])
            st["pos"] = seq_tile * tm + r0 + lax.broadcasted_iota(jnp.int32, (slice_rows, 1), 0)

        def diff_half(gi, hf):
            c0 = gi * group_dim + hf * half
            return _pool_diff(st["xn"], st["pos"], _POOL_WINDOWS[gi], slice(c0, c0 + half))

        def first_half(gi):
            st["diff"] = diff_half(gi, 0)

        def finish_group(gi):
            cols = slice(gi * group_dim, (gi + 1) * group_dim)
            diff = jnp.concatenate([st["diff"], diff_half(gi, 1)], axis=1)
            y = jnp.dot(diff.astype(_BF16), pw_ref[gi], preferred_element_type=_F32)
            x_buf[pl.ds(hl + r0, slice_rows), cols] = st["raw"][hl:, cols] + y * ps_ref[:, cols]

        pieces = [load_and_norm]
        for gi in range(len(_POOL_WINDOWS)):
            pieces += [functools.partial(first_half, gi), functools.partial(finish_group, gi)]
        return pieces

    @pl.when(jnp.logical_and(i == 0, f == 0))
    def _():
        for cp in x_copies(0):
            cp.start()
        for cp in x_copies(0):
            cp.wait()
        for s in reversed(range(n_slices)):
            for piece in pool_slice(0, s):
                piece()

    has_next = i + 1 < pl.num_programs(0)

    @pl.when(jnp.logical_and(f == 1, has_next))
    def _():
        for cp in x_copies(i + 1):
            cp.start()

    @pl.when(jnp.logical_and(f == first_pool_step - 1, has_next))
    def _():
        for cp in x_copies(i + 1):
            cp.wait()

    def chunk(xn, base):
        h = jnp.dot(xn, wu_ref[0].astype(_BF16), preferred_element_type=_F32)
        h = jnp.maximum(h, 0.0)
        h = (h * h).astype(_BF16)
        o_ref[...] = base + jnp.dot(h, wd_ref[0].astype(_BF16), preferred_element_type=_F32)

    @pl.when(f == 0)
    def _():
        x1 = x_buf[pl.ds(hl, tm), :]
        xn = (x1 * _inv_rms(x1) * g_ref[...]).astype(_BF16)
        xn_ref[...] = xn
        chunk(xn, x1)

    @pl.when(jnp.logical_and(f > 0, f < first_pool_step))
    def _():
        chunk(xn_ref[...], o_ref[...])

    @pl.when(f >= first_pool_step)
    def _():
        pool_pieces = pool_slice(i + 1, n_f - 1 - f)
        width = 2 * _LANES
        h = jnp.dot(xn_ref[...], wu_ref[0].astype(_BF16), preferred_element_type=_F32)
        h = jnp.maximum(h, 0.0)
        h = (h * h).astype(_BF16)
        pool_pieces.pop(0)()
        for c in range(0, o_ref.shape[1], width):
            o_ref[:, c:c + width] = o_ref[:, c:c + width] + jnp.dot(
                h, wd_ref[0, :, c:c + width].astype(_BF16), preferred_element_type=_F32)
            if pool_pieces:
                pool_pieces.pop(0)()
        for piece in pool_pieces:
            piece()


def _pool_mlp(x, pool_g, pool_w_bf16, pool_scale, norm_g, w_up, w_down, layer, *, tm=1024, tf=512,
              slice_rows=256):
    b, s, d = x.shape
    m = b * s
    d_ff = w_up.shape[2]
    n_groups = len(_POOL_WINDOWS)
    group_dim = d // n_groups
    n_slices = tm // slice_rows
    assert s % tm == 0 and tm % slice_rows == 0 and slice_rows % _POOL_HALO == 0
    assert d_ff // tf - n_slices >= 3
    return pl.pallas_call(
        functools.partial(_pool_mlp_kernel, tiles_per_seq=s // tm, slice_rows=slice_rows),
        grid=(m // tm, d_ff // tf),
        in_specs=[
            pl.BlockSpec(memory_space=pl.ANY),
            pl.BlockSpec((1, d), lambda i, f: (0, 0)),
            pl.BlockSpec((n_groups, group_dim, group_dim), lambda i, f: (0, 0, 0)),
            pl.BlockSpec((1, d), lambda i, f: (0, 0)),
            pl.BlockSpec((1, d), lambda i, f: (0, 0)),
            pl.BlockSpec((1, d, tf), lambda i, f: (layer, 0, f)),
            pl.BlockSpec((1, tf, d), lambda i, f: (layer, f, 0)),
        ],
        out_specs=pl.BlockSpec((tm, d), lambda i, f: (i, 0)),
        out_shape=jax.ShapeDtypeStruct((m, d), _F32),
        scratch_shapes=[
            pltpu.VMEM((tm, d), _BF16),
            pltpu.VMEM((_POOL_HALO + tm, d), _F32),
            pltpu.SemaphoreType.DMA((2,)),
        ],
        compiler_params=pltpu.CompilerParams(
            dimension_semantics=("arbitrary", "arbitrary"),
            vmem_limit_bytes=_VMEM_LIMIT_BYTES),
    )(x.reshape(m, d), pool_g.reshape(1, d), pool_w_bf16, pool_scale.reshape(1, d), norm_g.reshape(1, d),
      w_up, w_down)


def _mlp_kernel(x_ref, g_ref, wu_ref, wd_ref, *rest, final_norm):
    if final_norm:
        gf_ref, o_ref, xn_ref = rest
    else:
        o_ref, xn_ref = rest
    f = pl.program_id(1)

    def chunk(xn, base):
        h = jnp.dot(xn, wu_ref[0].astype(_BF16), preferred_element_type=_F32)
        h = jnp.maximum(h, 0.0)
        h = (h * h).astype(_BF16)
        o_ref[...] = base + jnp.dot(h, wd_ref[0].astype(_BF16), preferred_element_type=_F32)

    @pl.when(f == 0)
    def _():
        x = x_ref[...]
        xn = (x * _inv_rms(x) * g_ref[...]).astype(_BF16)
        xn_ref[...] = xn
        chunk(xn, x)

    @pl.when(f > 0)
    def _():
        chunk(xn_ref[...], o_ref[...])

    if final_norm:
        @pl.when(f == pl.num_programs(1) - 1)
        def _():
            r = o_ref[...]
            o_ref[...] = r * _inv_rms(r) * gf_ref[...]


def _mlp(x2d, norm_g, w_up, w_down, layer, final_g=None, *, tm=1024, tf=512):
    m, d = x2d.shape
    d_ff = w_up.shape[2]
    final_norm = final_g is not None
    in_specs = [
        pl.BlockSpec((tm, d), lambda i, f: (i, 0)),
        pl.BlockSpec((1, d), lambda i, f: (0, 0)),
        pl.BlockSpec((1, d, tf), lambda i, f: (layer, 0, f)),
        pl.BlockSpec((1, tf, d), lambda i, f: (layer, f, 0)),
    ]
    args = [x2d, norm_g.reshape(1, d), w_up, w_down]
    if final_norm:
        in_specs.append(pl.BlockSpec((1, d), lambda i, f: (0, 0)))
        args.append(final_g.reshape(1, d))
    return pl.pallas_call(
        functools.partial(_mlp_kernel, final_norm=final_norm),
        grid=(m // tm, d_ff // tf),
        in_specs=in_specs,
        out_specs=pl.BlockSpec((tm, d), lambda i, f: (i, 0)),
        out_shape=jax.ShapeDtypeStruct((m, d), _F32),
        scratch_shapes=[pltpu.VMEM((tm, d), _BF16)],
        compiler_params=pltpu.CompilerParams(
            dimension_semantics=("parallel", "arbitrary"),
            vmem_limit_bytes=_VMEM_LIMIT_BYTES),
    )(*args)


def _proj_kernel(x_ref, g_ref, wkv_ref, wq_ref, o_ref, inv_ref, *, q_part, q_scale):
    part = pl.program_id(1)

    @pl.when(part == 0)
    def _():
        inv_ref[...] = jnp.broadcast_to(_inv_rms(x_ref[...]), inv_ref.shape)

    x = x_ref[...]
    inv = inv_ref[...]
    lanes = inv.shape[1]
    xh = jnp.concatenate([x[:, c:c + lanes] * inv for c in range(0, x.shape[1], lanes)], axis=1)
    xn = (xh * g_ref[0]).astype(_BF16)
    w = jnp.where(part == q_part, wq_ref[...], wkv_ref[...])
    y = jnp.dot(xn, w, preferred_element_type=_F32)
    y = y * jnp.where(part == q_part, q_scale, 1.0)
    for h in range(o_ref.shape[0]):
        o_ref[h] = y[:, h * _HEAD_DIM:(h + 1) * _HEAD_DIM].astype(_BF16)


def _kvq_proj(x2d, gains, w_kv, w_q, q_scale, n_heads, *, tm=1024):
    m, d = x2d.shape
    n_parts = gains.shape[0]
    assert w_kv.shape == (d, (n_parts - 1) * d) and w_q.shape == (d, d) and d == n_heads * _HEAD_DIM
    return pl.pallas_call(
        functools.partial(_proj_kernel, q_part=n_parts - 1, q_scale=q_scale),
        grid=(m // tm, n_parts),
        in_specs=[
            pl.BlockSpec((tm, d), lambda i, n: (i, 0)),
            pl.BlockSpec((1, 1, d), lambda i, n: (n, 0, 0)),
            pl.BlockSpec((d, d), lambda i, n: (0, jnp.minimum(n, n_parts - 2))),
            pl.BlockSpec((d, d), lambda i, n: (0, 0), pipeline_mode=pl.Buffered(1)),
        ],
        out_specs=pl.BlockSpec((n_heads, tm, _HEAD_DIM), lambda i, n: (n, i, 0)),
        out_shape=jax.ShapeDtypeStruct((n_parts * n_heads, m, _HEAD_DIM), _BF16),
        scratch_shapes=[pltpu.VMEM((tm, _LANES), _F32)],
        compiler_params=pltpu.CompilerParams(
            dimension_semantics=("parallel", "arbitrary"),
            vmem_limit_bytes=_VMEM_LIMIT_BYTES),
    )(x2d, gains, w_kv, w_q)


def _log2_sigmoid(z2):
    return jnp.minimum(z2, 0.0) - jnp.log2(1.0 + jnp.exp2(-jnp.abs(z2)))


def _split_hi_lo(x):
    hi = x.astype(_BF16)
    lo = (x - hi.astype(_F32)).astype(_BF16)
    return jnp.concatenate([hi, lo], axis=1)


def _attn_kernel(q_ref, k_ref, v_ref, o_ref, lb_ref, lhs_ref, cs_ref, reach_ref, *, seq):
    t = _ATTN_BLOCK
    n_blocks = seq // t
    row = lax.broadcasted_iota(jnp.int32, (t, t), 0)
    col = lax.broadcasted_iota(jnp.int32, (t, t), 1)
    strictly_causal = col < row
    later = jnp.where(row > col, 1.0, 0.0).astype(_BF16)
    ones = jnp.ones((t, t), _BF16)
    sums = jnp.concatenate([jnp.concatenate([later, ones], axis=1)] * 2, axis=0)
    nt_dims = (((1,), (1,)), ((), ()))

    n_win = _ATTN_BACK_BLOCKS + 1

    def logs_phase(i, q0, n_back):
        w0 = q0 - n_back * t
        z_all = lax.dot_general(q_ref[0, pl.ds(q0, t), :], k_ref[0, pl.ds(w0, (n_back + 1) * t), :],
                                nt_dims, preferred_element_type=_F32)
        for jb in range(n_back + 1):
            z2 = z_all[:, (n_back - jb) * t:(n_back - jb + 1) * t]
            lb = _log2_sigmoid(z2)
            l1 = lb - z2
            if jb == 0:
                l1 = jnp.where(strictly_causal, l1, 0.0)
            lb_ref[i, :, jb * t:(jb + 1) * t] = lb
            lhs_ref[pl.ds((i * n_win + jb) * t, t), :] = _split_hi_lo(l1)
        for jb in range(n_back + 1, n_win):
            lhs_ref[pl.ds((i * n_win + jb) * t, t), :] = jnp.zeros((t, 2 * t), _BF16)

    def weights_phase(i, q0, n_back):
        w0 = q0 - n_back * t
        carry = None
        weights = []
        for jb in range(n_back + 1):
            rows = pl.ds((i * n_win + jb) * t, t)
            between = cs_ref[rows, :t] if carry is None else cs_ref[rows, :t] + carry
            a = jnp.exp2(lb_ref[i, :, jb * t:(jb + 1) * t] + between)
            if jb == 0:
                a = jnp.where(strictly_causal, a, 0.0)
            weights.append(a.astype(_BF16))
            carry = cs_ref[rows, t:] if carry is None else carry + cs_ref[rows, t:]
        a_all = jnp.concatenate(weights[::-1], axis=1)
        out = jnp.dot(a_all, v_ref[0, pl.ds(w0, (n_back + 1) * t), :], preferred_element_type=_F32)
        o_ref[0, pl.ds(q0, t), :] = out.astype(o_ref.dtype)
        reach_ref[i] = jnp.max(carry)

    def sums_phase(blocks):
        rows = slice(blocks[0] * n_win * t, (blocks[-1] + 1) * n_win * t)
        cs_ref[rows, :] = jnp.dot(lhs_ref[rows, :], sums, preferred_element_type=_F32)

    groups = [list(range(g, min(g + _ATTN_GROUP, n_blocks))) for g in range(0, n_blocks, _ATTN_GROUP)]
    for step in range(len(groups) + 2):
        if step >= 2:
            for i in groups[step - 2]:
                weights_phase(i, i * t, min(i, _ATTN_BACK_BLOCKS))
        if 1 <= step <= len(groups):
            sums_phase(groups[step - 1])
        if step < len(groups):
            for i in groups[step]:
                logs_phase(i, i * t, min(i, _ATTN_BACK_BLOCKS))

    def continue_block(i, _):
        @pl.when(reach_ref[i] > _LOG2_WEIGHT_FLOOR)
        def _():
            q0 = pl.multiple_of(i * t, t)
            q = q_ref[0, pl.ds(q0, t), :]
            carry = cs_ref[pl.ds(i * n_win * t, t), t:]
            for jb in range(1, n_win):
                carry = carry + cs_ref[pl.ds((i * n_win + jb) * t, t), t:]

            def cond(state):
                j, reach, _, _ = state
                return jnp.logical_and(j >= 0, reach > _LOG2_WEIGHT_FLOOR)

            def body(state):
                j, _, carry, acc = state
                k0 = pl.multiple_of(j * t, t)
                z2 = lax.dot_general(q, k_ref[0, pl.ds(k0, t), :], nt_dims, preferred_element_type=_F32)
                lb = _log2_sigmoid(z2)
                cs = jnp.dot(_split_hi_lo(lb - z2), sums, preferred_element_type=_F32)
                a = jnp.exp2(lb + cs[:, :t] + carry)
                acc = acc + jnp.dot(a.astype(_BF16), v_ref[0, pl.ds(k0, t), :],
                                    preferred_element_type=_F32)
                carry = carry + cs[:, t:]
                return j - 1, jnp.max(carry), carry, acc

            acc = o_ref[0, pl.ds(q0, t), :].astype(_F32)
            _, _, _, acc = lax.while_loop(cond, body, (i - n_win, reach_ref[i], carry, acc))
            o_ref[0, pl.ds(q0, t), :] = acc.astype(o_ref.dtype)

        return 0

    lax.fori_loop(n_win, n_blocks, continue_block, 0)


def _stick_breaking_attention(kvq, b, s, n_heads):
    t, n_win = _ATTN_BLOCK, _ATTN_BACK_BLOCKS + 1
    n_blocks = s // t
    assert s % t == 0 and n_blocks >= n_win
    return pl.pallas_call(
        functools.partial(_attn_kernel, seq=s),
        grid=(b, n_heads),
        in_specs=[
            pl.BlockSpec((1, s, _HEAD_DIM), lambda bi, h: (2 * n_heads + h, bi, 0)),
            pl.BlockSpec((1, s, _HEAD_DIM), lambda bi, h: (h, bi, 0)),
            pl.BlockSpec((1, s, _HEAD_DIM), lambda bi, h: (n_heads + h, bi, 0)),
        ],
        out_specs=pl.BlockSpec((1, s, _HEAD_DIM), lambda bi, h: (h, bi, 0)),
        out_shape=jax.ShapeDtypeStruct((n_heads, b * s, _HEAD_DIM), _BF16),
        scratch_shapes=[
            pltpu.VMEM((n_blocks, t, n_win * t), _F32),
            pltpu.VMEM((n_blocks * n_win * t, 2 * t), _BF16),
            pltpu.VMEM((n_blocks * n_win * t, 2 * t), _F32),
            pltpu.SMEM((n_blocks,), _F32),
        ],
        compiler_params=pltpu.CompilerParams(
            dimension_semantics=("parallel", "arbitrary"),
            vmem_limit_bytes=_VMEM_LIMIT_BYTES),
    )(kvq, kvq, kvq)


def _oproj_kernel(x_ref, o_ref_in, w_ref, out_ref):
    o = jnp.concatenate([o_ref_in[h] for h in range(o_ref_in.shape[0])], axis=1)
    out_ref[...] = x_ref[...] + jnp.dot(o, w_ref[...].astype(_BF16), preferred_element_type=_F32)


def _out_proj(x2d, o_heads, w_o, *, tm=512):
    m, d = x2d.shape
    n_heads = o_heads.shape[0]
    return pl.pallas_call(
        _oproj_kernel,
        grid=(m // tm,),
        in_specs=[
            pl.BlockSpec((tm, d), lambda i: (i, 0)),
            pl.BlockSpec((n_heads, tm, _HEAD_DIM), lambda i: (0, i, 0)),
            pl.BlockSpec((d, d), lambda i: (0, 0)),
        ],
        out_specs=pl.BlockSpec((tm, d), lambda i: (i, 0)),
        out_shape=jax.ShapeDtypeStruct((m, d), _F32),
        compiler_params=pltpu.CompilerParams(
            dimension_semantics=("parallel",),
            vmem_limit_bytes=_VMEM_LIMIT_BYTES),
    )(x2d, o_heads, w_o)


def kernel(x, pool_norm, pool_w, pool_scale, kv_norm, w_kv, attn_norm, w_q, w_o, mlp_norm,
           w_up, w_down, final_norm):
    b, s, d = x.shape
    n_heads = d // _HEAD_DIM
    m = b * s
    assert pool_norm.shape[0] == 1 and attn_norm.shape[0] == 1 and mlp_norm.shape[0] == 2

    x2 = _pool_mlp(x, pool_norm[0], pool_w[0].astype(_BF16), pool_scale[0], mlp_norm[0], w_up, w_down, 0)

    q_scale = math.log2(math.e) / math.sqrt(_HEAD_DIM)
    gains = jnp.stack([kv_norm, kv_norm, attn_norm[0]]).reshape(3, 1, d)
    kvq = _kvq_proj(x2, gains, w_kv.astype(_BF16), w_q[0].astype(_BF16), q_scale, n_heads)
    o = _stick_breaking_attention(kvq, b, s, n_heads)
    x3 = _out_proj(x2, o, w_o[0])

    out = _mlp(x3, mlp_norm[1], w_up, w_down, 1, final_norm)
    return out.reshape(b, s, d)
```
